```python
import jax
import jax.numpy as jnp
from jax import lax
import numpy as np

D_MODEL = 1024
BATCH = 4
SEQ = 8192
DEPTH = 4

GRID_W = 64
CTX_LEN = 256
N_BRANCH = 4
BRANCH_W = D_MODEL // 2
CONV_W = 3
MLSTM_HEADS = 4
MLSTM_HD = BRANCH_W // MLSTM_HEADS
MLSTM_CHUNK = 64
POOL_WINDOWS = (2, 4, 8, 16)
POOL_GW = BRANCH_W // len(POOL_WINDOWS)
ATT_HEAD_DIM = 64
ATT_Q_HEADS = BRANCH_W // ATT_HEAD_DIM
ATT_KV_HEADS = 2
ATT_WINDOW = 128
ATT_BLOCK = 128
ROPE_BASE = 10000.0
SC_W = BRANCH_W
D_FF = 2816
EPS = 1e-6

SPLITS = (BRANCH_W, BRANCH_W, BRANCH_W, BRANCH_W, 4 * MLSTM_HEADS,
          BRANCH_W,
          ATT_Q_HEADS * ATT_HEAD_DIM, ATT_KV_HEADS * ATT_HEAD_DIM, ATT_KV_HEADS * ATT_HEAD_DIM,
          SC_W, SC_W, SC_W,
          N_BRANCH * D_MODEL)
IN_WIDTH = sum(SPLITS)

kernel_name = 'hybrid_mlstm_pool_swa_conv_diffusion_block'


def split_cols(p):
    idx, acc = [], 0
    for w in SPLITS[:-1]:
        acc += w
        idx.append(acc)
    return jnp.split(p, idx, axis=-1)


def rmsnorm(x, g):
    xf = x.astype(jnp.float32)
    y = xf * lax.rsqrt(jnp.mean(xf * xf, axis=-1, keepdims=True) + EPS)
    return (y * g.astype(jnp.float32)).astype(x.dtype)


def modulate(h, shift, scale):
    return h * (1 + scale) + shift


def dwconv3(x, w):
    xp = jnp.pad(x, ((0, 0), (1, 1), (0, 0)))
    return xp[:, :-2] * w[0] + xp[:, 1:-1] * w[1] + xp[:, 2:] * w[2]


def axial_rope_tables(rows):
    row = jnp.repeat(jnp.arange(rows), GRID_W)
    col = jnp.tile(jnp.arange(GRID_W), rows)
    n_freq = ATT_HEAD_DIM // 4
    inv_freq = ROPE_BASE ** (-jnp.arange(n_freq, dtype=jnp.float32) / n_freq)
    ang = jnp.stack([row, col], axis=-1).astype(jnp.float32)[:, :, None] * inv_freq
    return jnp.cos(ang), jnp.sin(ang)


def apply_axial_rope(x, cos, sin):
    B, S, H, dh = x.shape
    xf = x.astype(jnp.float32).reshape(B, S, H, 2, 2, dh // 4)
    x1, x2 = xf[..., 0, :], xf[..., 1, :]
    c, s = cos[:, None], sin[:, None]
    out = jnp.stack([x1 * c - x2 * s, x1 * s + x2 * c], axis=-2)
    return out.reshape(B, S, H, dh).astype(x.dtype)


def mlstm_chunkwise(q, k, v, ig, lf, state):
    B, H, T, Dh = q.shape
    nc = T // MLSTM_CHUNK

    def chunks(a):
        return jnp.moveaxis(a.reshape(B, H, nc, MLSTM_CHUNK, *a.shape[3:]), 2, 0)

    tri = jnp.tril(jnp.ones((MLSTM_CHUNK, MLSTM_CHUNK), dtype=bool))

    def step(carry, inp):
        C, n, m = carry
        qc, kc, vc, ic, fc = inp
        b = jnp.cumsum(fc, axis=-1)
        a = b + m[..., None]
        dmat = jnp.where(tri, b[..., :, None] - b[..., None, :] + ic[..., None, :], -jnp.inf)
        mt = jnp.maximum(a, dmat.max(-1))
        inter = jnp.exp(a - mt)
        s = jnp.einsum('bhtd,bhsd->bhts', qc, kc) * jnp.exp(dmat - mt[..., None])
        num = jnp.einsum('bhts,bhsd->bhtd', s, vc) + inter[..., None] * jnp.einsum('bhtk,bhkv->bhtv', qc, C)
        den = s.sum(-1) + inter * jnp.einsum('bhtk,bhk->bht', qc, n)
        h = num / jnp.maximum(jnp.abs(den), jnp.exp(-mt))[..., None]
        m_new = mt[..., -1]
        decay = jnp.exp(b[..., -1] + m - m_new)
        wk = jnp.exp(b[..., -1:] - b + ic - m_new[..., None])
        C_new = decay[..., None, None] * C + jnp.einsum('bhs,bhsk,bhsv->bhkv', wk, kc, vc)
        n_new = decay[..., None] * n + jnp.einsum('bhs,bhsk->bhk', wk, kc)
        return (C_new, n_new, m_new), h

    state, hs = lax.scan(step, state, tuple(chunks(a) for a in (q, k, v, ig, lf)))
    return jnp.moveaxis(hs, 0, 2).reshape(B, H, T, Dh), state


def mlstm_inputs(q, k, v, g, qk_conv, gate_bias):
    B, T, _ = q.shape
    H, Dh = MLSTM_HEADS, MLSTM_HD
    qk = jax.nn.silu(dwconv3(jnp.concatenate([q, k], axis=-1), qk_conv)).astype(jnp.float32)
    heads = lambda a: a.reshape(B, T, H, Dh).transpose(0, 2, 1, 3)
    qh = heads(qk[..., :BRANCH_W])
    kh = heads(qk[..., BRANCH_W:]) * Dh ** -0.5
    vh = heads(v.astype(jnp.float32))
    gp = (g.astype(jnp.float32).reshape(B, T, 4, H) + gate_bias.astype(jnp.float32)).transpose(2, 0, 3, 1)
    return qh, kh, vh, gp[0:2], jax.nn.log_sigmoid(gp[2:4])


def mlstm_bidir(xin, cin, ctx_out):
    qx, kx, vx, ix, fx = xin
    qc, kc, vc, ic, fc = cin
    B, H, _, Dh = qx.shape
    zero = (jnp.zeros((B, H, Dh, Dh), jnp.float32), jnp.zeros((B, H, Dh), jnp.float32), jnp.zeros((B, H), jnp.float32))
    rev = lambda a: jnp.flip(a, axis=2)
    hc_f, st_f = mlstm_chunkwise(qc, kc, vc, ic[0], fc[0], zero)
    hc_b, st_b = mlstm_chunkwise(rev(qc), rev(kc), rev(vc), rev(ic[1]), rev(fc[1]), zero)
    hx_f, _ = mlstm_chunkwise(qx, kx, vx, ix[0], fx[0], st_f)
    hx_b, _ = mlstm_chunkwise(rev(qx), rev(kx), rev(vx), rev(ix[1]), rev(fx[1]), st_b)
    merge_heads = lambda a: a.transpose(0, 2, 1, 3).reshape(a.shape[0], a.shape[2], -1)
    hx = merge_heads(hx_f + rev(hx_b))
    hc = merge_heads(hc_f + rev(hc_b)) if ctx_out else None
    return hx, hc


def multiscale_pool(p, w_pool, scale):
    B, T, _ = p.shape
    G = len(POOL_WINDOWS)
    pf = p.astype(jnp.float32).reshape(B, T, G, POOL_GW)
    cs = jnp.pad(jnp.cumsum(pf, axis=1), ((0, 0), (1, 0), (0, 0), (0, 0)))
    half = jnp.array(POOL_WINDOWS) // 2
    t = jnp.arange(T)[:, None]
    lo = jnp.clip(t - half, 0, T)
    hi = jnp.clip(t + half, 0, T)
    g = jnp.arange(G)[None, :]
    mean = (cs[:, hi, g] - cs[:, lo, g]) / (hi - lo).astype(jnp.float32)[..., None]
    y = jnp.einsum('btgc,gcd->btgd', mean - pf, w_pool.astype(jnp.float32))
    return (y.reshape(B, T, -1) * scale.astype(jnp.float32)).astype(p.dtype)


def window_attention(q, k, v, k_ctx, v_ctx, sink):
    B, S, G, R, dh = q.shape
    nb = S // ATT_BLOCK
    scale = dh ** -0.5
    qb = q.reshape(B, nb, ATT_BLOCK, G, R, dh)

    def band(a):
        ab = jnp.pad(a.reshape(B, nb, ATT_BLOCK, G, dh), ((0, 0), (1, 1), (0, 0), (0, 0), (0, 0)))
        return jnp.concatenate([ab[:, :-2], ab[:, 1:-1], ab[:, 2:]], axis=2)

    kb, vb = band(k), band(v)
    blk = jnp.arange(nb)[:, None]
    qpos = blk * ATT_BLOCK + jnp.arange(ATT_BLOCK)[None]
    kpos = (blk - 1) * ATT_BLOCK + jnp.arange(3 * ATT_BLOCK)[None]
    valid = (jnp.abs(qpos[:, :, None] - kpos[:, None, :]) <= ATT_WINDOW) & ((kpos >= 0) & (kpos < S))[:, None, :]
    s_loc = jnp.einsum('bnqgrd,bnkgd->bngrqk', qb, kb).astype(jnp.float32) * scale
    s_loc = jnp.where(valid[None, :, None, None], s_loc, -jnp.inf)
    s_ctx = jnp.einsum('bnqgrd,blgd->bngrql', qb, k_ctx).astype(jnp.float32) * scale
    sk = sink.astype(jnp.float32).reshape(1, 1, G, R, 1)
    m = jnp.maximum(jnp.maximum(s_loc.max(-1), s_ctx.max(-1)), sk)
    e_loc = jnp.exp(s_loc - m[..., None])
    e_ctx = jnp.exp(s_ctx - m[..., None])
    den = e_loc.sum(-1) + e_ctx.sum(-1) + jnp.exp(sk - m)
    o = (jnp.einsum('bngrqk,bnkgd->bnqgrd', e_loc, vb.astype(jnp.float32))
         + jnp.einsum('bngrql,blgd->bnqgrd', e_ctx, v_ctx.astype(jnp.float32)))
    o = o / jnp.moveaxis(den, -1, 2)[..., None]
    return o.reshape(B, S, G * R * dh).astype(q.dtype)


def context_attention(q, k, v, sink):
    B, L, G, R, dh = q.shape
    s = jnp.einsum('blgrd,bmgd->bgrlm', q, k).astype(jnp.float32) * dh ** -0.5
    sk = jnp.broadcast_to(sink.astype(jnp.float32).reshape(1, G, R, 1, 1), s.shape[:-1] + (1,))
    p = jax.nn.softmax(jnp.concatenate([sk, s], axis=-1), axis=-1)[..., 1:]
    o = jnp.einsum('bgrlm,bmgd->blgrd', p, v.astype(jnp.float32))
    return o.reshape(B, L, G * R * dh).astype(q.dtype)


def merge_branches(p, h_mlstm, y_att, pool_w, pool_scale, sconv_w, w_branch, w_out):
    B, T, _ = h_mlstm.shape
    ya = (jax.nn.sigmoid(p[3]) * h_mlstm).astype(p[3].dtype)
    yb = multiscale_pool(p[5], pool_w, pool_scale)
    yd = p[9] * dwconv3(p[10] * p[11], sconv_w)
    ys = jnp.stack([ya, yb, y_att, yd], axis=2)
    proj = jnp.einsum('btnw,nwd->btnd', ys, w_branch)
    gates = jax.nn.sigmoid(p[12].reshape(B, T, N_BRANCH, -1))
    return jnp.einsum('btnd,btnd->btd', gates, proj) @ w_out


def token_mixers(ux, uc, w_in, qk_conv, gate_bias, pool_w, pool_scale, sink, sconv_w, w_branch, w_out, rope, ctx_out):
    B, S, _ = ux.shape
    L = uc.shape[1]
    px = split_cols(ux @ w_in)
    pc = split_cols(uc @ w_in)
    hx, hc = mlstm_bidir(mlstm_inputs(px[0], px[1], px[2], px[4], qk_conv, gate_bias),
                         mlstm_inputs(pc[0], pc[1], pc[2], pc[4], qk_conv, gate_bias), ctx_out)
    G, R, dh = ATT_KV_HEADS, ATT_Q_HEADS // ATT_KV_HEADS, ATT_HEAD_DIM
    k_ctx = pc[7].reshape(B, L, G, dh)
    v_ctx = pc[8].reshape(B, L, G, dh)
    q_x = apply_axial_rope(px[6].reshape(B, S, G * R, dh), *rope).reshape(B, S, G, R, dh)
    k_x = apply_axial_rope(px[7].reshape(B, S, G, dh), *rope)
    att_x = window_attention(q_x, k_x, px[8].reshape(B, S, G, dh), k_ctx, v_ctx, sink)
    out_x = merge_branches(px, hx, att_x, pool_w, pool_scale, sconv_w, w_branch, w_out)
    if not ctx_out:
        return out_x, None
    att_c = context_attention(pc[6].reshape(B, L, G, R, dh), k_ctx, v_ctx, sink)
    out_c = merge_branches(pc, hc, att_c, pool_w, pool_scale, sconv_w, w_branch, w_out)
    return out_x, out_c


def conv_ffn(u, w_up, w_conv, w_down):
    a = dwconv3(u @ w_up, w_conv)
    gate, val = jnp.split(a, 2, axis=-1)
    return (jax.nn.silu(gate) * val) @ w_down


def setup_inputs(seed: int = 0) -> dict:
    key = jax.random.key(seed)
    ks = jax.random.split(key, 20)
    D = D_MODEL
    nrm = lambda k, shape, s: jax.random.normal(k, shape, jnp.float32) * s
    x = nrm(ks[0], (BATCH, SEQ, D), 1.0)
    c = nrm(ks[1], (BATCH, D), 1.0)
    ctx = nrm(ks[2], (BATCH, CTX_LEN, D), 1.0)
    c_ctx = nrm(ks[3], (D,), 1.0)
    w_mod = nrm(ks[4], (DEPTH, D, 6 * D), 0.5 * D ** -0.5)
    b_mod = nrm(ks[5], (DEPTH, 6 * D), 0.01)
    norm_g = 1.0 + nrm(ks[6], (DEPTH, 4, D), 0.05)
    w_in = nrm(ks[7], (DEPTH, D, IN_WIDTH), D ** -0.5)
    mlstm_qk_conv = nrm(ks[8], (DEPTH, CONV_W, 2 * BRANCH_W), CONV_W ** -0.5)
    in_bias = nrm(ks[9], (DEPTH, 2, MLSTM_HEADS), 0.1)
    f_bias = jnp.linspace(3.0, 6.0, MLSTM_HEADS)[None, None] + nrm(ks[10], (DEPTH, 2, MLSTM_HEADS), 0.1)
    mlstm_gate_bias = jnp.concatenate([in_bias, f_bias], axis=1)
    pool_w = nrm(ks[11], (DEPTH, len(POOL_WINDOWS), POOL_GW, POOL_GW), POOL_GW ** -0.5)
    pool_scale = 1.0 + nrm(ks[12], (DEPTH, BRANCH_W), 0.05)
    attn_sink = nrm(ks[13], (DEPTH, ATT_Q_HEADS), 0.5)
    sconv_w = nrm(ks[14], (DEPTH, CONV_W, SC_W), CONV_W ** -0.5)
    w_branch = nrm(ks[15], (DEPTH, N_BRANCH, BRANCH_W, D), BRANCH_W ** -0.5)
    w_out = nrm(ks[16], (DEPTH, D, D), D ** -0.5)
    ffn_up = nrm(ks[17], (DEPTH, D, 2 * D_FF), D ** -0.5)
    ffn_conv = nrm(ks[18], (DEPTH, CONV_W, 2 * D_FF), CONV_W ** -0.5)
    ffn_down = nrm(ks[19], (DEPTH, D_FF, D), D_FF ** -0.5)
    return {'x': x, 'c': c, 'ctx': ctx, 'c_ctx': c_ctx, 'w_mod': w_mod, 'b_mod': b_mod, 'norm_g': norm_g,
            'w_in': w_in, 'mlstm_qk_conv': mlstm_qk_conv, 'mlstm_gate_bias': mlstm_gate_bias,
            'pool_w': pool_w, 'pool_scale': pool_scale, 'attn_sink': attn_sink, 'sconv_w': sconv_w,
            'w_branch': w_branch, 'w_out': w_out, 'ffn_up': ffn_up, 'ffn_conv': ffn_conv, 'ffn_down': ffn_down}


def reference(x, c, ctx, c_ctx, w_mod, b_mod, norm_g, w_in, mlstm_qk_conv, mlstm_gate_bias, pool_w, pool_scale,
              attn_sink, sconv_w, w_branch, w_out, ffn_up, ffn_conv, ffn_down):
    B, S, D = x.shape
    rows = S // GRID_W
    rope = axial_rope_tables(rows)
    sc_x = jax.nn.silu(c)[:, None, :]
    sc_c = jax.nn.silu(c_ctx)[None, None, :]
    h = ctx
    for l in range(DEPTH):
        ctx_out = l < DEPTH - 1
        mod_x = jnp.split(sc_x @ w_mod[l] + b_mod[l], 6, axis=-1)
        mod_c = jnp.split(sc_c @ w_mod[l] + b_mod[l], 6, axis=-1)
        ux = modulate(rmsnorm(x, norm_g[l, 0]), mod_x[0], mod_x[1])
        uc = modulate(rmsnorm(h, norm_g[l, 0]), mod_c[0], mod_c[1])
        mx, mc = token_mixers(ux, uc, w_in[l], mlstm_qk_conv[l], mlstm_gate_bias[l], pool_w[l], pool_scale[l],
                              attn_sink[l], sconv_w[l], w_branch[l], w_out[l], rope, ctx_out)
        x = x + mod_x[2] * rmsnorm(mx, norm_g[l, 1])
        fx = conv_ffn(modulate(rmsnorm(x, norm_g[l, 2]), mod_x[3], mod_x[4]), ffn_up[l], ffn_conv[l], ffn_down[l])
        x = x + mod_x[5] * rmsnorm(fx, norm_g[l, 3])
        if ctx_out:
            h = h + mod_c[2] * rmsnorm(mc, norm_g[l, 1])
            fc = conv_ffn(modulate(rmsnorm(h, norm_g[l, 2]), mod_c[3], mod_c[4]), ffn_up[l], ffn_conv[l], ffn_down[l])
            h = h + mod_c[5] * rmsnorm(fc, norm_g[l, 3])
    return x
```

```python
import functools

import jax
import jax.numpy as jnp
from jax import lax
from jax.experimental import pallas as pl
from jax.experimental.pallas import tpu as pltpu

F32 = jnp.float32
BF16 = jnp.bfloat16
HIGHEST = lax.Precision.HIGHEST

D_MODEL = 1024
GRID_W = 64
N_BRANCH = 4
BRANCH_W = D_MODEL // 2
MLSTM_HEADS = 4
MLSTM_HD = BRANCH_W // MLSTM_HEADS
POOL_HALF = (1, 2, 4, 8)
POOL_GW = BRANCH_W // len(POOL_HALF)
ATT_HEAD_DIM = 64
ATT_Q_HEADS = BRANCH_W // ATT_HEAD_DIM
ATT_KV_HEADS = 2
ATT_BLOCK = 128
ROPE_BASE = 10000.0
D_FF = 2816
EPS = 1e-6

LANES = 128
HALO = 8
FF_CHUNK = 256
VMEM_LIMIT = 56 * 1024 * 1024

_OFF = {}
_acc = 0
for _name, _w in (("q", 512), ("k", 512), ("v", 512), ("o", 512), ("pool", 512), ("aq", 512),
                  ("akv", 256), ("sb", 512), ("sc", 512), ("sx", 512)):
    _OFF[_name] = _acc
    _acc += _w
W1_COLS = _acc
_GATE_PERM = (0, 1, 2, 3, 8, 9, 10, 11, 4, 5, 6, 7, 12, 13, 14, 15)


def _iota(shape, dim):
    return lax.broadcasted_iota(jnp.int32, shape, dim)


def _rms(x):
    return x * lax.rsqrt(jnp.mean(x * x, axis=-1, keepdims=True) + EPS)


def _sigmoid(x):
    return 1.0 / (1.0 + jnp.exp(-x))


def _nt(a, b, **kw):
    return lax.dot_general(a, b, (((1,), (1,)), ((), ())), preferred_element_type=F32, **kw)


def _tn(a, b, **kw):
    return lax.dot_general(a, b, (((0,), (0,)), ((), ())), preferred_element_type=F32, **kw)


def _mm(a, b, **kw):
    return jnp.dot(a, b, preferred_element_type=F32, **kw)


def _params(*sem):
    return pltpu.CompilerParams(dimension_semantics=sem, vmem_limit_bytes=VMEM_LIMIT)


def _const_spec(shape, index_map):
    return pl.BlockSpec(shape, index_map, pipeline_mode=pl.Buffered(1))


def _mod_kernel(c_ref, w_ref, b_ref, o_ref):
    cond = c_ref[...]
    sc = cond * _sigmoid(cond)
    o_ref[0] = _mm(sc, w_ref[0], precision=HIGHEST) + b_ref[0]


def _modulation(cond, w_mod, b_mod):
    depth, d, n = w_mod.shape
    rows = cond.shape[0]
    tn = 1024
    return pl.pallas_call(
        _mod_kernel,
        grid=(depth, n // tn),
        in_specs=[pl.BlockSpec((rows, d), lambda l, j: (0, 0)),
                  pl.BlockSpec((1, d, tn), lambda l, j: (l, 0, j)),
                  pl.BlockSpec((1, 1, tn), lambda l, j: (l, 0, j))],
        out_specs=pl.BlockSpec((1, rows, tn), lambda l, j: (l, 0, j)),
        out_shape=jax.ShapeDtypeStruct((depth, rows, n), F32),
        compiler_params=_params("parallel", "parallel"),
        name="modulation",
    )(cond, w_mod, b_mod.reshape(depth, 1, n))


def _halo_specs(tm, d, seq):
    nh = seq // HALO
    per = tm // HALO
    main = pl.BlockSpec((1, tm, d), lambda b, i: (b, i, 0))
    prev = pl.BlockSpec((1, HALO, d), lambda b, i: (b, jnp.maximum(i * per - 1, 0), 0))
    nxt = pl.BlockSpec((1, HALO, d), lambda b, i: (b, jnp.minimum((i + 1) * per, nh - 1), 0))
    return main, prev, nxt


def _modulated_tile(xm_ref, xp_ref, xn_ref, gain, shift, scale, tm, seq):
    i = pl.program_id(1)
    xx = jnp.concatenate([xp_ref[0], xm_ref[0], xn_ref[0]], axis=0)
    u = _rms(xx) * gain * (1.0 + scale) + shift
    pos = i * tm - HALO + _iota((tm + 2 * HALO, 1), 0)
    return jnp.where((pos >= 0) & (pos < seq), u, 0.0)


def _conv3(r_ref, w_ref, cols, tm):
    return (r_ref[HALO - 1:HALO - 1 + tm, :] * w_ref[0, 0:1, cols]
            + r_ref[HALO:HALO + tm, :] * w_ref[0, 1:2, cols]
            + r_ref[HALO + 1:HALO + 1 + tm, :] * w_ref[0, 2:3, cols])


def _rope(r, cos, sin):
    lane = _iota((1, LANES), 1)
    first = (lane % 32) < 16
    swapped = jnp.where(first, pltpu.roll(r, LANES - 16, 1), pltpu.roll(r, 16, 1))
    return r * cos + swapped * sin


def _inproj_kernel(xm_ref, xp_ref, xn_ref, mod_ref, g_ref, w_ref, wg_ref, gb_ref, qkc_ref, scw_ref, pw_ref,
                   ps_ref, cos_ref, sin_ref,
                   qk_out, v_out, og_out, gates_out, yb_out, aq_out, akv_out, yd_out,
                   u_scr, um_scr, r_scr, *, tm, seq):
    i = pl.program_id(1)
    u = _modulated_tile(xm_ref, xp_ref, xn_ref, g_ref[0, 0:1, :], mod_ref[0, 0, 0:1, :], mod_ref[0, 0, 1:2, :],
                        tm, seq)
    u_scr[...] = u.astype(BF16)
    um_scr[...] = u[HALO:HALO + tm].astype(BF16)

    def wcols(name, width, extra=0):
        o = _OFF[name] + extra
        return w_ref[0, :, o:o + width]

    for c, name in enumerate(("q", "k")):
        cols = slice(c * BRANCH_W, (c + 1) * BRANCH_W)
        r_scr[...] = _mm(u_scr[...], wcols(name, BRANCH_W))
        a = _conv3(r_scr, qkc_ref, cols, tm)
        a = a * _sigmoid(a)
        if name == "k":
            a = a * MLSTM_HD ** -0.5
        qk_out[0, :, cols] = a.astype(BF16)

    um = um_scr[...]
    v_out[0] = _mm(um, wcols("v", BRANCH_W)).astype(BF16)
    og_out[0] = _sigmoid(_mm(um, wcols("o", BRANCH_W))).astype(BF16)

    g = _nt(wg_ref[0], um) + gb_ref[0]
    logsig = jnp.minimum(g, 0.0) - jnp.log(1.0 + jnp.exp(-jnp.abs(g)))
    is_forget = (_iota((16, 1), 0) % 8) >= MLSTM_HEADS
    g = jnp.where(is_forget, logsig, g)
    gates_out[0, 0] = g[0:8]
    gates_out[0, 1] = g[8:16]

    r_scr[...] = _mm(u_scr[...], wcols("pool", BRANCH_W))
    t = i * tm + _iota((tm, 1), 0)
    for gi, half in enumerate(POOL_HALF):
        cols = slice(gi * POOL_GW, (gi + 1) * POOL_GW)
        acc = r_scr[HALO - half:HALO - half + tm, cols]
        for o in range(-half + 1, half):
            acc = acc + r_scr[HALO + o:HALO + o + tm, cols]
        cnt = (jnp.minimum(t + half, seq) - jnp.maximum(t - half, 0)).astype(F32)
        delta = acc / cnt - r_scr[HALO:HALO + tm, cols]
        y = _mm(delta.astype(BF16), pw_ref[0, gi]) * ps_ref[0, 0:1, cols]
        yb_out[0, :, cols] = y.astype(BF16)

    cos = cos_ref[...]
    sin = sin_ref[...]
    rq = _mm(um, wcols("aq", BRANCH_W))
    for j in range(BRANCH_W // LANES):
        cols = slice(j * LANES, (j + 1) * LANES)
        aq_out[0, :, cols] = (_rope(rq[:, cols], cos, sin) * ATT_HEAD_DIM ** -0.5).astype(BF16)
    rkv = _mm(um, wcols("akv", 2 * LANES))
    akv_out[0, :, 0:LANES] = _rope(rkv[:, 0:LANES], cos, sin).astype(BF16)
    akv_out[0, :, LANES:2 * LANES] = rkv[:, LANES:2 * LANES].astype(BF16)

    rb = _mm(um, wcols("sb", BRANCH_W))
    r_scr[...] = _mm(u_scr[...], wcols("sc", BRANCH_W)) * _mm(u_scr[...], wcols("sx", BRANCH_W))
    yd_out[0] = (rb * _conv3(r_scr, scw_ref, slice(0, BRANCH_W), tm)).astype(BF16)


def _inproj(x, mod, norm_g, w1, wg, gb, qkc, scw, pw, ps, cos, sin, *, layer, mod_row, tm):
    bsz, seq, d = x.shape
    nt = seq // tm
    main, prev, nxt = _halo_specs(tm, d, seq)
    mrow = (lambda b: b) if mod_row is None else (lambda b: mod_row)
    lsel3 = lambda b, i: (layer, 0, 0)
    in_specs = [
        main, prev, nxt,
        pl.BlockSpec((1, 1, 6, d), lambda b, i: (layer, mrow(b), 0, 0)),
        _const_spec((1, 4, d), lsel3),
        _const_spec((1, d, W1_COLS), lsel3),
        _const_spec((1, 16, d), lsel3),
        _const_spec((1, 16, 1), lsel3),
        _const_spec((1, 3, 2 * BRANCH_W), lsel3),
        _const_spec((1, 3, BRANCH_W), lsel3),
        _const_spec((1, len(POOL_HALF), POOL_GW, POOL_GW), lambda b, i: (layer, 0, 0, 0)),
        _const_spec((1, 1, BRANCH_W), lsel3),
        pl.BlockSpec((tm, LANES), lambda b, i: (i, 0)),
        pl.BlockSpec((tm, LANES), lambda b, i: (i, 0)),
    ]
    tok = lambda w: pl.BlockSpec((1, tm, w), lambda b, i: (b, i, 0))
    out_specs = [tok(2 * BRANCH_W), tok(BRANCH_W), tok(BRANCH_W),
                 pl.BlockSpec((1, 2, 8, tm), lambda b, i: (b, 0, 0, i)),
                 tok(BRANCH_W), tok(BRANCH_W), tok(2 * LANES), tok(BRANCH_W)]
    act = lambda w: jax.ShapeDtypeStruct((bsz, seq, w), BF16)
    out_shape = [act(2 * BRANCH_W), act(BRANCH_W), act(BRANCH_W),
                 jax.ShapeDtypeStruct((bsz, 2, 8, seq), F32),
                 act(BRANCH_W), act(BRANCH_W), act(2 * LANES), act(BRANCH_W)]
    return pl.pallas_call(
        functools.partial(_inproj_kernel, tm=tm, seq=seq),
        grid=(bsz, nt),
        in_specs=in_specs,
        out_specs=out_specs,
        out_shape=out_shape,
        scratch_shapes=[pltpu.VMEM((tm + 2 * HALO, d), BF16), pltpu.VMEM((tm, d), BF16),
                        pltpu.VMEM((tm + 2 * HALO, BRANCH_W), F32)],
        compiler_params=_params("parallel", "parallel"),
        name="inproj",
    )(x, x, x, mod, norm_g, w1, wg, gb, qkc, scw, pw, ps, cos, sin)


def _mlstm_kernel(qk_ref, v_ref, g_ref, c0_ref, m0_ref, h_out, c_out, m_out, *, chunk):
    L = chunk
    hd = MLSTM_HD
    fwd = pl.program_id(0) == 0

    @pl.when(pl.program_id(2) == 0)
    def _():
        c_out[...] = c0_ref[...]
        m_out[...] = m0_ref[...]

    g = g_ref[0, 0]
    row = _iota((L, L), 0)
    col = _iota((L, L), 1)
    order = jnp.where(fwd, 1, -1) * (col - row)
    seen = order <= 0
    seen_t = order >= 0
    eye = (row == col).astype(F32)
    cum_col = _nt(seen.astype(F32), g, precision=HIGHEST)
    g_col = _nt(eye, g, precision=HIGHEST)
    cum_row = _mm(g, seen_t.astype(F32), precision=HIGHEST)
    is_last = _iota((L, 1), 0) == jnp.where(fwd, L - 1, 0)
    ones = jnp.ones((L, hd), BF16)
    neg_inf = -jnp.inf

    for h in range(MLSTM_HEADS):
        cols = slice(h * hd, (h + 1) * hd)
        q = qk_ref[0, :, cols]
        k = qk_ref[0, :, BRANCH_W + h * hd:BRANCH_W + (h + 1) * hd]
        vaug = jnp.concatenate([v_ref[0, :, cols], ones], axis=1)
        b_col = cum_col[:, MLSTM_HEADS + h:MLSTM_HEADS + h + 1]
        ig_col = g_col[:, h:h + 1]
        row_term = cum_row[MLSTM_HEADS + h:MLSTM_HEADS + h + 1, :] - g[h:h + 1, :]
        m_prev = m_out[0, 0, h:h + 1, 0:1]
        dmat = jnp.where(seen, b_col - row_term, neg_inf)
        a = b_col + m_prev
        mt = jnp.maximum(a, jnp.max(dmat, axis=-1, keepdims=True))
        s = _nt(q, k) * jnp.exp(dmat - mt)
        inter = jnp.exp(a - mt)
        c_prev = c_out[0, 0, h]
        r = _mm(s.astype(BF16), vaug) + inter * _mm(q, c_prev.astype(BF16))
        den = jnp.maximum(jnp.abs(r[:, hd:]), jnp.exp(-mt))
        h_out[0, 0, :, cols] = (r[:, :hd] / den).astype(BF16)
        m_new = jnp.max(jnp.where(is_last, mt, neg_inf), axis=0, keepdims=True)
        b_tot = jnp.max(jnp.where(is_last, b_col, neg_inf), axis=0, keepdims=True)
        decay = jnp.exp(b_tot + m_prev - m_new)
        wk = jnp.exp(b_tot - b_col + ig_col - m_new)
        c_out[0, 0, h] = decay * c_prev + _tn(k, (wk * vaug.astype(F32)).astype(BF16))
        m_out[0, 0, h:h + 1, :] = jnp.broadcast_to(m_new, (1, LANES))


def _mlstm(qk, v, gates, c0, m0, *, chunk):
    bsz, seq, _ = qk.shape
    nc = seq // chunk
    hd = MLSTM_HD
    cidx = lambda d, j: j + d * (nc - 1 - 2 * j)
    state_c = pl.BlockSpec((1, 1, MLSTM_HEADS, hd, 2 * hd), lambda d, b, j: (d, b, 0, 0, 0))
    state_m = pl.BlockSpec((1, 1, 8, LANES), lambda d, b, j: (d, b, 0, 0))
    return pl.pallas_call(
        functools.partial(_mlstm_kernel, chunk=chunk),
        grid=(2, bsz, nc),
        in_specs=[pl.BlockSpec((1, chunk, 2 * BRANCH_W), lambda d, b, j: (b, cidx(d, j), 0)),
                  pl.BlockSpec((1, chunk, BRANCH_W), lambda d, b, j: (b, cidx(d, j), 0)),
                  pl.BlockSpec((1, 1, 8, chunk), lambda d, b, j: (b, d, 0, cidx(d, j))),
                  state_c, state_m],
        out_specs=[pl.BlockSpec((1, 1, chunk, BRANCH_W), lambda d, b, j: (d, b, cidx(d, j), 0)),
                   state_c, state_m],
        out_shape=[jax.ShapeDtypeStruct((2, bsz, seq, BRANCH_W), BF16),
                   jax.ShapeDtypeStruct((2, bsz, MLSTM_HEADS, hd, 2 * hd), F32),
                   jax.ShapeDtypeStruct((2, bsz, 8, LANES), F32)],
        compiler_params=_params("parallel", "parallel", "arbitrary"),
        name="mlstm",
    )(qk, v, gates, c0, m0)


def _attn_kernel(*refs, nb, local):
    if local:
        q_ref, kvp_ref, kvc_ref, kvn_ref, kvx_ref, sink_ref, o_ref = refs
        kv = jnp.concatenate([kvp_ref[0], kvc_ref[0], kvn_ref[0], kvx_ref[0]], axis=0)
    else:
        q_ref, kvx_ref, sink_ref, o_ref = refs
        kv = kvx_ref[0]
    n = pl.program_id(1)
    blk = ATT_BLOCK
    nk = kv.shape[0]
    kv = kv.astype(F32)
    k2 = kv[:, 0:LANES]
    v2 = kv[:, LANES:2 * LANES]
    k2r = pltpu.roll(k2, LANES // 2, 1)
    v2r = pltpu.roll(v2, LANES // 2, 1)
    lo = _iota((1, LANES), 1) < LANES // 2

    if local:
        qi = _iota((blk, blk), 0)
        kj = _iota((blk, blk), 1)
        valid = jnp.concatenate([(kj >= qi) & (n > 0), jnp.ones((blk, blk), jnp.bool_),
                                 (kj <= qi) & (n < nb - 1), jnp.ones((blk, nk - 3 * blk), jnp.bool_)], axis=1)
        bias = jnp.where(valid, 0.0, -jnp.inf)
        bias = jnp.concatenate([bias, bias], axis=0)

    for g in range(ATT_KV_HEADS):
        ka, kb = (k2, k2r) if g == 0 else (k2r, k2)
        va, vb = (v2, v2r) if g == 0 else (v2r, v2)
        kbd = jnp.concatenate([jnp.where(lo, ka, 0.0), jnp.where(lo, 0.0, kb)], axis=0).astype(BF16)
        vbd = jnp.concatenate([jnp.where(lo, va, 0.0), jnp.where(lo, 0.0, vb)], axis=0).astype(BF16)
        t0 = 2 * g * LANES
        q = jnp.concatenate([q_ref[0, :, t0:t0 + LANES], q_ref[0, :, t0 + LANES:t0 + 2 * LANES]], axis=0)
        s = _nt(q, kbd)
        es, dens = [], []
        for c in range(2):
            sc = s[:, c * nk:(c + 1) * nk]
            if local:
                sc = sc + bias
            h0 = 4 * g + c
            sk = jnp.concatenate([jnp.broadcast_to(sink_ref[0, h0:h0 + 1, 0:1], (blk, 1)),
                                  jnp.broadcast_to(sink_ref[0, h0 + 2:h0 + 3, 0:1], (blk, 1))], axis=0)
            m = jnp.maximum(jnp.max(sc, axis=-1, keepdims=True), sk)
            e = jnp.exp(sc - m)
            dens.append(jnp.sum(e, axis=-1, keepdims=True) + jnp.exp(sk - m))
            es.append(e.astype(BF16))
        o = _mm(jnp.concatenate(es, axis=1), vbd) / jnp.where(lo, dens[0], dens[1])
        o_ref[0, :, t0:t0 + LANES] = o[0:blk].astype(BF16)
        o_ref[0, :, t0 + LANES:t0 + 2 * LANES] = o[blk:2 * blk].astype(BF16)


def _attention(aq, akv, akv_ctx, sink, *, layer, local):
    bsz, seq, _ = aq.shape
    nb = seq // ATT_BLOCK
    lc = akv_ctx.shape[1]
    qspec = pl.BlockSpec((1, ATT_BLOCK, BRANCH_W), lambda b, n: (b, n, 0))
    ctx_spec = pl.BlockSpec((1, lc, 2 * LANES), lambda b, n: (b, 0, 0))
    sink_spec = pl.BlockSpec((1, 8, LANES), lambda b, n: (layer, 0, 0))
    if local:
        kvs = [pl.BlockSpec((1, ATT_BLOCK, 2 * LANES), lambda b, n: (b, jnp.maximum(n - 1, 0), 0)),
               pl.BlockSpec((1, ATT_BLOCK, 2 * LANES), lambda b, n: (b, n, 0)),
               pl.BlockSpec((1, ATT_BLOCK, 2 * LANES), lambda b, n: (b, jnp.minimum(n + 1, nb - 1), 0))]
        in_specs = [qspec, *kvs, ctx_spec, sink_spec]
        args = (aq, akv, akv, akv, akv_ctx, sink)
    else:
        in_specs = [qspec, ctx_spec, sink_spec]
        args = (aq, akv_ctx, sink)
    return pl.pallas_call(
        functools.partial(_attn_kernel, nb=nb, local=local),
        grid=(bsz, nb),
        in_specs=in_specs,
        out_specs=qspec,
        out_shape=jax.ShapeDtypeStruct((bsz, seq, BRANCH_W), BF16),
        compiler_params=_params("parallel", "parallel"),
        name="attention",
    )(*args)


def _merge_kernel(x_ref, mod_ref, g_ref, hf_ref, hb_ref, og_ref, yb_ref, att_ref, yd_ref, wg_ref, wb_ref, wo_ref,
                  o_ref):
    d = D_MODEL
    x = x_ref[0]
    u = (_rms(x) * g_ref[0, 0:1, :] * (1.0 + mod_ref[0, 0, 1:2, :]) + mod_ref[0, 0, 0:1, :]).astype(BF16)
    ya = (og_ref[0].astype(F32) * (hf_ref[0, 0].astype(F32) + hb_ref[0, 0].astype(F32))).astype(BF16)
    ys = (ya, yb_ref[0], att_ref[0], yd_ref[0])
    merged = None
    for n in range(N_BRANCH):
        gate = _sigmoid(_mm(u, wg_ref[0, :, n * d:(n + 1) * d]))
        term = gate * _mm(ys[n], wb_ref[0, n])
        merged = term if merged is None else merged + term
    out = _mm(merged.astype(BF16), wo_ref[0])
    o_ref[0] = x + mod_ref[0, 0, 2:3, :] * (_rms(out) * g_ref[0, 1:2, :])


def _merge(x, mod, norm_g, h, og, yb, att, yd, w_mg, w_branch, w_out, *, layer, mod_row, tm):
    bsz, seq, d = x.shape
    mrow = (lambda b: b) if mod_row is None else (lambda b: mod_row)
    tok = lambda w: pl.BlockSpec((1, tm, w), lambda b, i: (b, i, 0))
    hspec = lambda dr: pl.BlockSpec((1, 1, tm, BRANCH_W), lambda b, i: (dr, b, i, 0))
    lsel3 = lambda b, i: (layer, 0, 0)
    return pl.pallas_call(
        _merge_kernel,
        grid=(bsz, seq // tm),
        in_specs=[tok(d),
                  pl.BlockSpec((1, 1, 6, d), lambda b, i: (layer, mrow(b), 0, 0)),
                  _const_spec((1, 4, d), lsel3),
                  hspec(0), hspec(1), tok(BRANCH_W), tok(BRANCH_W), tok(BRANCH_W), tok(BRANCH_W),
                  _const_spec((1, d, N_BRANCH * d), lsel3),
                  _const_spec((1, N_BRANCH, BRANCH_W, d), lambda b, i: (layer, 0, 0, 0)),
                  _const_spec((1, d, d), lsel3)],
        out_specs=tok(d),
        out_shape=jax.ShapeDtypeStruct((bsz, seq, d), F32),
        compiler_params=_params("parallel", "parallel"),
        name="merge",
    )(x, mod, norm_g, h, h, og, yb, att, yd, w_mg, w_branch, w_out)


def _ffn_kernel(xm_ref, xp_ref, xn_ref, mod_ref, g_ref, wu_ref, wc_ref, wd_ref, o_ref, u_scr, r_scr, h_scr, *,
                tm, seq):
    u = _modulated_tile(xm_ref, xp_ref, xn_ref, g_ref[0, 2:3, :], mod_ref[0, 0, 3:4, :], mod_ref[0, 0, 4:5, :],
                        tm, seq)
    u_scr[...] = u.astype(BF16)
    cw = FF_CHUNK
    for c in range(D_FF // cw):
        cols = slice(2 * c * cw, 2 * (c + 1) * cw)
        r_scr[...] = _mm(u_scr[...], wu_ref[0, :, cols])
        a = _conv3(r_scr, wc_ref, cols, tm)
        gate = a[:, :cw]
        h_scr[:, c * cw:(c + 1) * cw] = (gate * _sigmoid(gate) * a[:, cw:]).astype(BF16)
    out = _mm(h_scr[...], wd_ref[0])
    x = xm_ref[0]
    o_ref[0] = x + mod_ref[0, 0, 5:6, :] * (_rms(out) * g_ref[0, 3:4, :])


def _ffn(x, mod, norm_g, w_up, w_conv, w_down, *, layer, mod_row, tm):
    bsz, seq, d = x.shape
    main, prev, nxt = _halo_specs(tm, d, seq)
    mrow = (lambda b: b) if mod_row is None else (lambda b: mod_row)
    lsel3 = lambda b, i: (layer, 0, 0)
    return pl.pallas_call(
        functools.partial(_ffn_kernel, tm=tm, seq=seq),
        grid=(bsz, seq // tm),
        in_specs=[main, prev, nxt,
                  pl.BlockSpec((1, 1, 6, d), lambda b, i: (layer, mrow(b), 0, 0)),
                  _const_spec((1, 4, d), lsel3),
                  _const_spec((1, d, 2 * D_FF), lsel3),
                  _const_spec((1, 3, 2 * D_FF), lsel3),
                  _const_spec((1, D_FF, d), lsel3)],
        out_specs=main,
        out_shape=jax.ShapeDtypeStruct((bsz, seq, d), F32),
        scratch_shapes=[pltpu.VMEM((tm + 2 * HALO, d), BF16), pltpu.VMEM((tm + 2 * HALO, 2 * FF_CHUNK), F32),
                        pltpu.VMEM((tm, D_FF), BF16)],
        compiler_params=_params("parallel", "parallel"),
        name="convffn",
    )(x, x, x, mod, norm_g, w_up, w_conv, w_down)


def _rope_tables(seq):
    t = jnp.arange(seq, dtype=jnp.int32)[:, None]
    lane = jnp.arange(LANES, dtype=jnp.int32)[None, :]
    in_head = lane % ATT_HEAD_DIM
    n_freq = ATT_HEAD_DIM // 4
    inv_freq = ROPE_BASE ** (-(in_head % n_freq).astype(F32) / n_freq)
    pos = jnp.where(in_head < ATT_HEAD_DIM // 2, t // GRID_W, t % GRID_W).astype(F32)
    ang = pos * inv_freq
    sign = jnp.where((in_head % (2 * n_freq)) < n_freq, -1.0, 1.0)
    return jnp.cos(ang), sign * jnp.sin(ang)


def _interleave_ff(w):
    lead = w.shape[:-1]
    w = w.reshape(*lead, 2, D_FF // FF_CHUNK, FF_CHUNK)
    return jnp.swapaxes(w, -3, -2).reshape(*lead, 2 * D_FF)


def _tile_rows(seq):
    return min(seq, 512)


def kernel(x, c, ctx, c_ctx, w_mod, b_mod, norm_g, w_in, mlstm_qk_conv, mlstm_gate_bias, pool_w, pool_scale,
           attn_sink, sconv_w, w_branch, w_out, ffn_up, ffn_conv, ffn_down):
    bsz, seq, d = x.shape
    lc = ctx.shape[1]
    depth = w_mod.shape[0]
    assert d == D_MODEL and seq % 512 == 0 and lc % 256 == 0 and bsz + 1 <= 8

    cond = jnp.concatenate([c, c_ctx[None, :], jnp.zeros((8 - bsz - 1, d), F32)], axis=0)
    mod = _modulation(cond, w_mod, b_mod).reshape(depth, 8, 6, d)

    g0 = BRANCH_W * 4
    g1 = g0 + 4 * MLSTM_HEADS
    mg0 = w_in.shape[-1] - N_BRANCH * d
    w1 = jnp.concatenate([w_in[:, :, :g0], w_in[:, :, g1:mg0]], axis=-1).astype(BF16)
    perm = jnp.array(_GATE_PERM)
    wg = jnp.swapaxes(w_in[:, :, g0:g1][:, :, perm], 1, 2).astype(BF16)
    gb = mlstm_gate_bias.reshape(depth, 4 * MLSTM_HEADS)[:, perm][:, :, None]
    w_mg = w_in[:, :, mg0:].astype(BF16)
    pw = pool_w.astype(BF16)
    ps = pool_scale[:, None, :]
    sink = jnp.broadcast_to(attn_sink[:, :, None], (depth, ATT_Q_HEADS, LANES))
    wb = w_branch.astype(BF16)
    wo = w_out.astype(BF16)
    wu = _interleave_ff(ffn_up).astype(BF16)
    wc = _interleave_ff(ffn_conv)
    wd = ffn_down.astype(BF16)

    cos_x, sin_x = _rope_tables(seq)
    cos_c, sin_c = jnp.ones((lc, LANES), F32), jnp.zeros((lc, LANES), F32)

    hd = MLSTM_HD
    zero_c = jnp.zeros((2, bsz, MLSTM_HEADS, hd, 2 * hd), F32)
    zero_m = jnp.zeros((2, bsz, 8, LANES), F32)
    tm_x, tm_c = _tile_rows(seq), _tile_rows(lc)
    chunk = 256

    h = ctx
    for l in range(depth):
        ctx_out = l < depth - 1
        proj = functools.partial(_inproj, mod=mod, norm_g=norm_g, w1=w1, wg=wg, gb=gb, qkc=mlstm_qk_conv,
                                 scw=sconv_w, pw=pw, ps=ps, layer=l)
        qk_c, v_c, og_c, gt_c, yb_c, aq_c, akv_c, yd_c = proj(h, cos=cos_c, sin=sin_c, mod_row=bsz, tm=tm_c)
        qk_x, v_x, og_x, gt_x, yb_x, aq_x, akv_x, yd_x = proj(x, cos=cos_x, sin=sin_x, mod_row=None, tm=tm_x)
        h_c, st_c, st_m = _mlstm(qk_c, v_c, gt_c, zero_c, zero_m, chunk=chunk)
        h_x, _, _ = _mlstm(qk_x, v_x, gt_x, st_c, st_m, chunk=chunk)
        att_x = _attention(aq_x, akv_x, akv_c, sink, layer=l, local=True)
        tail = dict(norm_g=norm_g, layer=l)
        x = _merge(x, mod, h=h_x, og=og_x, yb=yb_x, att=att_x, yd=yd_x, w_mg=w_mg, w_branch=wb, w_out=wo,
                   mod_row=None, tm=tm_x, **tail)
        x = _ffn(x, mod, w_up=wu, w_conv=wc, w_down=wd, mod_row=None, tm=tm_x, **tail)
        if ctx_out:
            att_c = _attention(aq_c, akv_c, akv_c, sink, layer=l, local=False)
            h = _merge(h, mod, h=h_c, og=og_c, yb=yb_c, att=att_c, yd=yd_c, w_mg=w_mg, w_branch=wb, w_out=wo,
                       mod_row=bsz, tm=tm_c, **tail)
            h = _ffn(h, mod, w_up=wu, w_conv=wc, w_down=wd, mod_row=bsz, tm=tm_c, **tail)
    return x
```

```python
import functools

import jax
import jax.numpy as jnp
from jax import lax
from jax.experimental import pallas as pl
from jax.experimental.pallas import tpu as pltpu

F32 = jnp.float32
BF16 = jnp.bfloat16
HIGHEST = lax.Precision.HIGHEST

D_MODEL = 1024
GRID_W = 64
N_BRANCH = 4
BRANCH_W = D_MODEL // 2
MLSTM_HEADS = 4
MLSTM_HD = BRANCH_W // MLSTM_HEADS
MLSTM_CHUNK = 256
MLSTM_CHUNKS_PER_STEP = 4
POOL_HALF = (1, 2, 4, 8)
POOL_GW = BRANCH_W // len(POOL_HALF)
ATT_HEAD_DIM = 64
ATT_Q_HEADS = BRANCH_W // ATT_HEAD_DIM
ATT_KV_HEADS = 2
ATT_BLOCK = 128
ATT_SUB = 4
ROPE_BASE = 10000.0
D_FF = 2816
EPS = 1e-6

LANES = 128
HALO = 8
FF_CHUNK = 256
VMEM_LIMIT = 56 * 1024 * 1024

_OFF = {}
_acc = 0
for _name, _w in (("q", 512), ("k", 512), ("pool", 512), ("aq", 512), ("ak", 128), ("sb", 512), ("sc", 512),
                  ("sx", 512)):
    _OFF[_name] = _acc
    _acc += _w
W1_COLS = _acc
_TOFF = {}
_acc = 0
for _name, _w in (("v", 512), ("o", 512), ("av", 128), ("gates", 16)):
    _TOFF[_name] = _acc
    _acc += _w
WT_ROWS = _acc
_GATE_PERM = (0, 1, 2, 3, 8, 9, 10, 11, 4, 5, 6, 7, 12, 13, 14, 15)


def _iota(shape, dim):
    return lax.broadcasted_iota(jnp.int32, shape, dim)


def _rms(x):
    return x * lax.rsqrt(jnp.mean(x * x, axis=-1, keepdims=True) + EPS)


def _sigmoid(x):
    return 1.0 / (1.0 + jnp.exp(-x))


def _nt(a, b, **kw):
    return lax.dot_general(a, b, (((1,), (1,)), ((), ())), preferred_element_type=F32, **kw)


def _tn(a, b, **kw):
    return lax.dot_general(a, b, (((0,), (0,)), ((), ())), preferred_element_type=F32, **kw)


def _mm(a, b, **kw):
    return jnp.dot(a, b, preferred_element_type=F32, **kw)


def _params(*sem):
    return pltpu.CompilerParams(dimension_semantics=sem, vmem_limit_bytes=VMEM_LIMIT)


def _const_spec(shape, index_map):
    return pl.BlockSpec(shape, index_map, pipeline_mode=pl.Buffered(1))


def _mod_kernel(c_ref, w_ref, b_ref, o_ref):
    cond = c_ref[...]
    sc = cond * _sigmoid(cond)
    o_ref[0] = _mm(sc, w_ref[0], precision=HIGHEST) + b_ref[0]


def _modulation(cond, w_mod, b_mod):
    depth, d, n = w_mod.shape
    rows = cond.shape[0]
    tn = 1024
    return pl.pallas_call(
        _mod_kernel,
        grid=(depth, n // tn),
        in_specs=[pl.BlockSpec((rows, d), lambda l, j: (0, 0)),
                  pl.BlockSpec((1, d, tn), lambda l, j: (l, 0, j)),
                  pl.BlockSpec((1, 1, tn), lambda l, j: (l, 0, j))],
        out_specs=pl.BlockSpec((1, rows, tn), lambda l, j: (l, 0, j)),
        out_shape=jax.ShapeDtypeStruct((depth, rows, n), F32),
        compiler_params=_params("parallel", "parallel"),
        name="modulation",
    )(cond, w_mod, b_mod.reshape(depth, 1, n))


def _halo_specs(tm, d, seq):
    nh = seq // HALO
    per = tm // HALO
    main = pl.BlockSpec((1, tm, d), lambda b, i: (b, i, 0))
    prev = pl.BlockSpec((1, HALO, d), lambda b, i: (b, jnp.maximum(i * per - 1, 0), 0))
    nxt = pl.BlockSpec((1, HALO, d), lambda b, i: (b, jnp.minimum((i + 1) * per, nh - 1), 0))
    return main, prev, nxt


def _modulated_tile(xm_ref, xp_ref, xn_ref, gain, shift, scale, tm, seq):
    i = pl.program_id(1)
    xx = jnp.concatenate([xp_ref[0], xm_ref[0], xn_ref[0]], axis=0)
    u = _rms(xx) * gain * (1.0 + scale) + shift
    pos = i * tm - HALO + _iota((tm + 2 * HALO, 1), 0)
    return jnp.where((pos >= 0) & (pos < seq), u, 0.0)


def _conv3(r_ref, w_ref, cols, tm):
    return (r_ref[HALO - 1:HALO - 1 + tm, :] * w_ref[0, 0:1, cols]
            + r_ref[HALO:HALO + tm, :] * w_ref[0, 1:2, cols]
            + r_ref[HALO + 1:HALO + 1 + tm, :] * w_ref[0, 2:3, cols])


def _rope(r, cos, sin):
    lane = _iota((1, LANES), 1)
    first = (lane % 32) < 16
    swapped = jnp.where(first, pltpu.roll(r, LANES - 16, 1), pltpu.roll(r, 16, 1))
    return r * cos + swapped * sin


def _inproj_kernel(xm_ref, xp_ref, xn_ref, mod_ref, g_ref, w_ref, wt_ref, gb_ref, qkc_ref, scw_ref, pw_ref,
                   ps_ref, cos_ref, sin_ref,
                   qk_out, vt_out, ogt_out, gates_out, yb_out, aq_out, ak_out, avt_out, yd_out,
                   u_scr, um_scr, r_scr, *, tm, seq):
    i = pl.program_id(1)
    u = _modulated_tile(xm_ref, xp_ref, xn_ref, g_ref[0, 0:1, :], mod_ref[0, 0, 0:1, :], mod_ref[0, 0, 1:2, :],
                        tm, seq)
    u_scr[...] = u.astype(BF16)
    um_scr[...] = u[HALO:HALO + tm].astype(BF16)

    def wcols(name, width, extra=0):
        o = _OFF[name] + extra
        return w_ref[0, :, o:o + width]

    for c, name in enumerate(("q", "k")):
        cols = slice(c * BRANCH_W, (c + 1) * BRANCH_W)
        r_scr[...] = _mm(u_scr[...], wcols(name, BRANCH_W))
        a = _conv3(r_scr, qkc_ref, cols, tm)
        a = a * _sigmoid(a)
        if name == "k":
            a = a * MLSTM_HD ** -0.5
        qk_out[0, :, cols] = a.astype(BF16)

    um = um_scr[...]

    def trows(name, height):
        o = _TOFF[name]
        return wt_ref[0, o:o + height, :]

    vt_out[0] = _nt(trows("v", BRANCH_W), um).astype(BF16)
    ogt_out[0] = _sigmoid(_nt(trows("o", BRANCH_W), um)).astype(BF16)
    avt_out[0] = _nt(trows("av", LANES), um).astype(BF16)

    g = _nt(trows("gates", 16), um) + gb_ref[0]
    logsig = jnp.minimum(g, 0.0) - jnp.log(1.0 + jnp.exp(-jnp.abs(g)))
    is_forget = (_iota((16, 1), 0) % 8) >= MLSTM_HEADS
    g = jnp.where(is_forget, logsig, g)
    gates_out[0, 0] = g[0:8]
    gates_out[0, 1] = g[8:16]

    r_scr[...] = _mm(u_scr[...], wcols("pool", BRANCH_W))
    t = i * tm + _iota((tm, 1), 0)
    for gi, half in enumerate(POOL_HALF):
        cols = slice(gi * POOL_GW, (gi + 1) * POOL_GW)
        acc = r_scr[HALO - half:HALO - half + tm, cols]
        for o in range(-half + 1, half):
            acc = acc + r_scr[HALO + o:HALO + o + tm, cols]
        cnt = (jnp.minimum(t + half, seq) - jnp.maximum(t - half, 0)).astype(F32)
        delta = acc / cnt - r_scr[HALO:HALO + tm, cols]
        y = _mm(delta.astype(BF16), pw_ref[0, gi]) * ps_ref[0, 0:1, cols]
        yb_out[0, :, cols] = y.astype(BF16)

    cos = cos_ref[...]
    sin = sin_ref[...]
    rq = _mm(um, wcols("aq", BRANCH_W))
    for j in range(BRANCH_W // LANES):
        cols = slice(j * LANES, (j + 1) * LANES)
        aq_out[0, :, cols] = (_rope(rq[:, cols], cos, sin) * ATT_HEAD_DIM ** -0.5).astype(BF16)
    ak_out[0] = _rope(_mm(um, wcols("ak", LANES)), cos, sin).astype(BF16)

    rb = _mm(um, wcols("sb", BRANCH_W))
    r_scr[...] = _mm(u_scr[...], wcols("sc", BRANCH_W)) * _mm(u_scr[...], wcols("sx", BRANCH_W))
    yd_out[0] = (rb * _conv3(r_scr, scw_ref, slice(0, BRANCH_W), tm)).astype(BF16)


def _inproj(x, mod, norm_g, w1, wt, gb, qkc, scw, pw, ps, cos, sin, *, layer, mod_row, tm):
    bsz, seq, d = x.shape
    nt = seq // tm
    main, prev, nxt = _halo_specs(tm, d, seq)
    mrow = (lambda b: b) if mod_row is None else (lambda b: mod_row)
    lsel3 = lambda b, i: (layer, 0, 0)
    in_specs = [
        main, prev, nxt,
        pl.BlockSpec((1, 1, 6, d), lambda b, i: (layer, mrow(b), 0, 0)),
        _const_spec((1, 4, d), lsel3),
        _const_spec((1, d, W1_COLS), lsel3),
        _const_spec((1, WT_ROWS, d), lsel3),
        _const_spec((1, 16, 1), lsel3),
        _const_spec((1, 3, 2 * BRANCH_W), lsel3),
        _const_spec((1, 3, BRANCH_W), lsel3),
        _const_spec((1, len(POOL_HALF), POOL_GW, POOL_GW), lambda b, i: (layer, 0, 0, 0)),
        _const_spec((1, 1, BRANCH_W), lsel3),
        pl.BlockSpec((tm, LANES), lambda b, i: (i, 0)),
        pl.BlockSpec((tm, LANES), lambda b, i: (i, 0)),
    ]
    tok = lambda w: pl.BlockSpec((1, tm, w), lambda b, i: (b, i, 0))
    feat = lambda w: pl.BlockSpec((1, w, tm), lambda b, i: (b, 0, i))
    out_specs = [tok(2 * BRANCH_W), feat(BRANCH_W), feat(BRANCH_W),
                 pl.BlockSpec((1, 2, 8, tm), lambda b, i: (b, 0, 0, i)),
                 tok(BRANCH_W), tok(BRANCH_W), tok(LANES), feat(LANES), tok(BRANCH_W)]
    act = lambda w: jax.ShapeDtypeStruct((bsz, seq, w), BF16)
    act_t = lambda w: jax.ShapeDtypeStruct((bsz, w, seq), BF16)
    out_shape = [act(2 * BRANCH_W), act_t(BRANCH_W), act_t(BRANCH_W),
                 jax.ShapeDtypeStruct((bsz, 2, 8, seq), F32),
                 act(BRANCH_W), act(BRANCH_W), act(LANES), act_t(LANES), act(BRANCH_W)]
    return pl.pallas_call(
        functools.partial(_inproj_kernel, tm=tm, seq=seq),
        grid=(bsz, nt),
        in_specs=in_specs,
        out_specs=out_specs,
        out_shape=out_shape,
        scratch_shapes=[pltpu.VMEM((tm + 2 * HALO, d), BF16), pltpu.VMEM((tm, d), BF16),
                        pltpu.VMEM((tm + 2 * HALO, BRANCH_W), F32)],
        compiler_params=_params("parallel", "parallel"),
        name="inproj",
    )(x, x, x, mod, norm_g, w1, wt, gb, qkc, scw, pw, ps, cos, sin)


def _split3(x):
    x = jnp.concatenate([x, jnp.zeros_like(x)], axis=0)
    hi = x.astype(BF16)
    r1 = x - hi.astype(F32)
    mid = r1.astype(BF16)
    lo = (r1 - mid.astype(F32)).astype(BF16)
    return jnp.concatenate([hi, mid, lo], axis=0)


def _sum3(y):
    return (y[0:8] + y[16:24]) + y[32:40]


def _mlstm_kernel(qk_ref, vt_ref, g_ref, c0_ref, m0_ref, h_out, c_out, m_out, *, chunk, reverse):
    L = chunk
    hd = MLSTM_HD
    nck = qk_ref.shape[1] // L

    @pl.when(pl.program_id(1) == 0)
    def _():
        c_out[...] = c0_ref[...]
        m_out[...] = m0_ref[...]

    s_idx = _iota((L, L), 0)
    t_idx = _iota((L, L), 1)
    seen = (s_idx >= t_idx) if reverse else (s_idx <= t_idx)
    seen_t = (s_idx <= t_idx) if reverse else (s_idx >= t_idx)
    bias = jnp.where(seen, 0.0, -jnp.inf)
    seen_bf = seen.astype(BF16)
    seen_t_bf = seen_t.astype(BF16)
    eye = (s_idx == t_idx).astype(BF16)
    is_last = _iota((1, L), 1) == (0 if reverse else L - 1)
    ones = jnp.ones((hd, L), BF16)
    neg_inf = -jnp.inf

    for i in range(nck):
        sub = (nck - 1 - i) if reverse else i
        tok = slice(sub * L, (sub + 1) * L)
        g = g_ref[0, 0, :, tok]
        g3 = _split3(g)
        cum_row = _sum3(_mm(g3, seen_bf))
        col3 = [_nt(seen_t_bf, g3[16 * j:16 * (j + 1)]) for j in range(3)]
        gcol3 = [_nt(eye, g3[16 * j:16 * (j + 1)]) for j in range(3)]
        cum_col = (col3[0] + col3[1]) + col3[2]
        g_col = (gcol3[0] + gcol3[1]) + gcol3[2]

        for h in range(MLSTM_HEADS):
            rows = slice(h * hd, (h + 1) * hd)
            q = qk_ref[0, tok, rows]
            k = qk_ref[0, tok, BRANCH_W + h * hd:BRANCH_W + (h + 1) * hd]
            vaug = jnp.concatenate([vt_ref[0, rows, tok], ones], axis=0)
            b_row = cum_row[MLSTM_HEADS + h:MLSTM_HEADS + h + 1, :]
            ig_row = g[h:h + 1, :]
            key_col = cum_col[:, MLSTM_HEADS + h:MLSTM_HEADS + h + 1] - g_col[:, h:h + 1]
            m_prev = m_out[0, h:h + 1, 0:1]
            dmat = (b_row - key_col) + bias
            a = b_row + m_prev
            mt = jnp.maximum(a, jnp.max(dmat, axis=0, keepdims=True))
            p = _nt(k, q) * jnp.exp(dmat - mt)
            inter = jnp.exp(a - mt)
            c_prev = c_out[0, h]
            r = _mm(vaug, p.astype(BF16)) + inter * _nt(c_prev.astype(BF16), q)
            den = jnp.maximum(jnp.abs(r[hd:]), jnp.exp(-mt))
            h_out[0, rows, tok] = (r[:hd] / den).astype(BF16)
            m_new = jnp.max(jnp.where(is_last, mt, neg_inf), axis=1, keepdims=True)
            b_tot = jnp.max(jnp.where(is_last, b_row, neg_inf), axis=1, keepdims=True)
            decay = jnp.exp(b_tot + m_prev - m_new)
            wk = jnp.exp(b_tot - b_row + ig_row - m_new)
            c_out[0, h] = decay * c_prev + _mm((vaug.astype(F32) * wk).astype(BF16), k)
            m_out[0, h:h + 1, :] = jnp.broadcast_to(m_new, (1, LANES))


def _mlstm(qk, vt, gates, c0, m0, *, chunk, reverse):
    bsz, seq, _ = qk.shape
    nck = min(MLSTM_CHUNKS_PER_STEP, seq // chunk)
    tb = nck * chunk
    ns = seq // tb
    hd = MLSTM_HD
    direction = 1 if reverse else 0
    blk = (lambda j: ns - 1 - j) if reverse else (lambda j: j)
    state_c = pl.BlockSpec((1, MLSTM_HEADS, 2 * hd, hd), lambda b, j: (b, 0, 0, 0))
    state_m = pl.BlockSpec((1, 8, LANES), lambda b, j: (b, 0, 0))
    return pl.pallas_call(
        functools.partial(_mlstm_kernel, chunk=chunk, reverse=reverse),
        grid=(bsz, ns),
        in_specs=[pl.BlockSpec((1, tb, 2 * BRANCH_W), lambda b, j: (b, blk(j), 0)),
                  pl.BlockSpec((1, BRANCH_W, tb), lambda b, j: (b, 0, blk(j))),
                  pl.BlockSpec((1, 1, 8, tb), lambda b, j: (b, direction, 0, blk(j))),
                  state_c, state_m],
        out_specs=[pl.BlockSpec((1, BRANCH_W, tb), lambda b, j: (b, 0, blk(j))),
                   state_c, state_m],
        out_shape=[jax.ShapeDtypeStruct((bsz, BRANCH_W, seq), BF16),
                   jax.ShapeDtypeStruct((bsz, MLSTM_HEADS, 2 * hd, hd), F32),
                   jax.ShapeDtypeStruct((bsz, 8, LANES), F32)],
        compiler_params=_params("parallel", "arbitrary"),
        name="mlstm_bwd" if reverse else "mlstm_fwd",
    )(qk, vt, gates, c0, m0)


def _attn_kernel(*refs, nb, local):
    blk = ATT_BLOCK
    hd = ATT_HEAD_DIM
    if local:
        q_ref, kp_ref, kc_ref, kn_ref, kx_ref, vp_ref, vc_ref, vn_ref, vx_ref, sink_ref, o_ref = refs
        k_loc = jnp.concatenate([kp_ref[0], kc_ref[0], kn_ref[0]], axis=0)
        vt_loc = jnp.concatenate([vp_ref[0], vc_ref[0], vn_ref[0]], axis=1)
    else:
        q_ref, kx_ref, vx_ref, sink_ref, o_ref = refs
    nsub = q_ref.shape[1] // blk
    step = pl.program_id(1)
    lo = _iota((1, LANES), 1) < hd
    nq = ATT_Q_HEADS * blk
    sk = sink_ref[0]
    if local:
        kj = _iota((blk, nq), 0)
        qi = _iota((blk, nq), 1) % blk
        after = kj >= qi
        before = kj <= qi

    for i in range(nsub):
        if local:
            k = jnp.concatenate([k_loc[i * blk:(i + 3) * blk], kx_ref[0]], axis=0)
            vt = jnp.concatenate([vt_loc[:, i * blk:(i + 3) * blk], vx_ref[0]], axis=1)
        else:
            k = kx_ref[0]
            vt = vx_ref[0]
        nk = k.shape[0]
        va = jnp.concatenate([vt, jnp.ones((16, nk), BF16)], axis=0)
        parts = []
        for j in range(BRANCH_W // LANES):
            q2 = q_ref[0, i * blk:(i + 1) * blk, j * LANES:(j + 1) * LANES]
            zero = jnp.zeros_like(q2)
            parts += [jnp.where(lo, q2, zero), jnp.where(lo, zero, q2)]
        s = _nt(k, jnp.concatenate(parts, axis=0))
        if local:
            n = step * nsub + i
            bias_prev = jnp.where(after & (n > 0), 0.0, -jnp.inf)
            bias_next = jnp.where(before & (n < nb - 1), 0.0, -jnp.inf)
            s = jnp.concatenate([s[0:blk] + bias_prev, s[blk:2 * blk], s[2 * blk:3 * blk] + bias_next,
                                 s[3 * blk:]], axis=0)
        m = jnp.maximum(jnp.max(s, axis=0, keepdims=True), sk)
        r = _mm(va, jnp.exp(s - m).astype(BF16))
        den = r[2 * hd:2 * hd + 1, :] + jnp.exp(sk - m)
        for p in range(ATT_Q_HEADS):
            c = p % 2
            head = p // 2 + 4 * c
            cols = slice(p * blk, (p + 1) * blk)
            o_ref[0, head * hd:(head + 1) * hd, i * blk:(i + 1) * blk] = (
                r[c * hd:(c + 1) * hd, cols] / den[:, cols]).astype(BF16)


def _attention(aq, ak, avt, ak_ctx, avt_ctx, sink, *, layer, local):
    bsz, seq, _ = aq.shape
    nb = seq // ATT_BLOCK
    nsub = min(ATT_SUB, nb)
    tq = nsub * ATT_BLOCK
    lc = ak_ctx.shape[1]
    qspec = pl.BlockSpec((1, tq, BRANCH_W), lambda b, n: (b, n, 0))
    kx_spec = pl.BlockSpec((1, lc, LANES), lambda b, n: (b, 0, 0))
    vx_spec = pl.BlockSpec((1, LANES, lc), lambda b, n: (b, 0, 0))
    sink_spec = pl.BlockSpec((1, 1, ATT_Q_HEADS * ATT_BLOCK), lambda b, n: (layer, 0, 0))
    if local:
        prev = lambda n: jnp.maximum(n * nsub - 1, 0)
        nxt = lambda n: jnp.minimum((n + 1) * nsub, nb - 1)
        kspecs = [pl.BlockSpec((1, ATT_BLOCK, LANES), lambda b, n: (b, prev(n), 0)),
                  pl.BlockSpec((1, tq, LANES), lambda b, n: (b, n, 0)),
                  pl.BlockSpec((1, ATT_BLOCK, LANES), lambda b, n: (b, nxt(n), 0))]
        vspecs = [pl.BlockSpec((1, LANES, ATT_BLOCK), lambda b, n: (b, 0, prev(n))),
                  pl.BlockSpec((1, LANES, tq), lambda b, n: (b, 0, n)),
                  pl.BlockSpec((1, LANES, ATT_BLOCK), lambda b, n: (b, 0, nxt(n)))]
        in_specs = [qspec, *kspecs, kx_spec, *vspecs, vx_spec, sink_spec]
        args = (aq, ak, ak, ak, ak_ctx, avt, avt, avt, avt_ctx, sink)
    else:
        in_specs = [qspec, kx_spec, vx_spec, sink_spec]
        args = (aq, ak_ctx, avt_ctx, sink)
    return pl.pallas_call(
        functools.partial(_attn_kernel, nb=nb, local=local),
        grid=(bsz, nb // nsub),
        in_specs=in_specs,
        out_specs=pl.BlockSpec((1, BRANCH_W, tq), lambda b, n: (b, 0, n)),
        out_shape=jax.ShapeDtypeStruct((bsz, BRANCH_W, seq), BF16),
        compiler_params=_params("parallel", "parallel"),
        name="attention",
    )(*args)


def _merge_kernel(x_ref, mod_ref, g_ref, hf_ref, hb_ref, og_ref, yb_ref, att_ref, yd_ref, wg_ref, wb_ref, wo_ref,
                  o_ref):
    d = D_MODEL
    x = x_ref[0]
    u = (_rms(x) * g_ref[0, 0:1, :] * (1.0 + mod_ref[0, 0, 1:2, :]) + mod_ref[0, 0, 0:1, :]).astype(BF16)
    ya_t = (og_ref[0].astype(F32) * (hf_ref[0].astype(F32) + hb_ref[0].astype(F32))).astype(BF16)
    ys = (ya_t, yb_ref[0], att_ref[0], yd_ref[0])
    feature_major = (True, False, True, False)
    merged = None
    for n in range(N_BRANCH):
        gate = _sigmoid(_mm(u, wg_ref[0, :, n * d:(n + 1) * d]))
        proj = _tn(ys[n], wb_ref[0, n]) if feature_major[n] else _mm(ys[n], wb_ref[0, n])
        term = gate * proj
        merged = term if merged is None else merged + term
    out = _mm(merged.astype(BF16), wo_ref[0])
    o_ref[0] = x + mod_ref[0, 0, 2:3, :] * (_rms(out) * g_ref[0, 1:2, :])


def _merge(x, mod, norm_g, hf, hb, og, yb, att, yd, w_mg, w_branch, w_out, *, layer, mod_row, tm):
    bsz, seq, d = x.shape
    mrow = (lambda b: b) if mod_row is None else (lambda b: mod_row)
    tok = lambda w: pl.BlockSpec((1, tm, w), lambda b, i: (b, i, 0))
    feat = pl.BlockSpec((1, BRANCH_W, tm), lambda b, i: (b, 0, i))
    lsel3 = lambda b, i: (layer, 0, 0)
    return pl.pallas_call(
        _merge_kernel,
        grid=(bsz, seq // tm),
        in_specs=[tok(d),
                  pl.BlockSpec((1, 1, 6, d), lambda b, i: (layer, mrow(b), 0, 0)),
                  _const_spec((1, 4, d), lsel3),
                  feat, feat, feat, tok(BRANCH_W), feat, tok(BRANCH_W),
                  _const_spec((1, d, N_BRANCH * d), lsel3),
                  _const_spec((1, N_BRANCH, BRANCH_W, d), lambda b, i: (layer, 0, 0, 0)),
                  _const_spec((1, d, d), lsel3)],
        out_specs=tok(d),
        out_shape=jax.ShapeDtypeStruct((bsz, seq, d), F32),
        compiler_params=_params("parallel", "parallel"),
        name="merge",
    )(x, mod, norm_g, hf, hb, og, yb, att, yd, w_mg, w_branch, w_out)


def _ffn_kernel(xm_ref, xp_ref, xn_ref, mod_ref, g_ref, wu_ref, wc_ref, wd_ref, o_ref, u_scr, rg_scr, rv_scr,
                h_scr, *, tm, seq):
    u = _modulated_tile(xm_ref, xp_ref, xn_ref, g_ref[0, 2:3, :], mod_ref[0, 0, 3:4, :], mod_ref[0, 0, 4:5, :],
                        tm, seq)
    u_scr[...] = u.astype(BF16)
    cw = FF_CHUNK
    for c in range(D_FF // cw):
        gcols = slice(c * cw, (c + 1) * cw)
        vcols = slice(D_FF + c * cw, D_FF + (c + 1) * cw)
        rg_scr[...] = _mm(u_scr[...], wu_ref[0, :, gcols])
        rv_scr[...] = _mm(u_scr[...], wu_ref[0, :, vcols])
        gate = _conv3(rg_scr, wc_ref, gcols, tm)
        val = _conv3(rv_scr, wc_ref, vcols, tm)
        h_scr[:, gcols] = (gate * _sigmoid(gate) * val).astype(BF16)
    out = _mm(h_scr[...], wd_ref[0])
    x = xm_ref[0]
    o_ref[0] = x + mod_ref[0, 0, 5:6, :] * (_rms(out) * g_ref[0, 3:4, :])


def _ffn(x, mod, norm_g, w_up, w_conv, w_down, *, layer, mod_row, tm):
    bsz, seq, d = x.shape
    main, prev, nxt = _halo_specs(tm, d, seq)
    mrow = (lambda b: b) if mod_row is None else (lambda b: mod_row)
    lsel3 = lambda b, i: (layer, 0, 0)
    return pl.pallas_call(
        functools.partial(_ffn_kernel, tm=tm, seq=seq),
        grid=(bsz, seq // tm),
        in_specs=[main, prev, nxt,
                  pl.BlockSpec((1, 1, 6, d), lambda b, i: (layer, mrow(b), 0, 0)),
                  _const_spec((1, 4, d), lsel3),
                  _const_spec((1, d, 2 * D_FF), lsel3),
                  _const_spec((1, 3, 2 * D_FF), lsel3),
                  _const_spec((1, D_FF, d), lsel3)],
        out_specs=main,
        out_shape=jax.ShapeDtypeStruct((bsz, seq, d), F32),
        scratch_shapes=[pltpu.VMEM((tm + 2 * HALO, d), BF16), pltpu.VMEM((tm + 2 * HALO, FF_CHUNK), F32),
                        pltpu.VMEM((tm + 2 * HALO, FF_CHUNK), F32),
                        pltpu.VMEM((tm, D_FF), BF16)],
        compiler_params=_params("parallel", "parallel"),
        name="convffn",
    )(x, x, x, mod, norm_g, w_up, w_conv, w_down)


def _rope_tables(seq):
    t = jnp.arange(seq, dtype=jnp.int32)[:, None]
    lane = jnp.arange(LANES, dtype=jnp.int32)[None, :]
    in_head = lane % ATT_HEAD_DIM
    n_freq = ATT_HEAD_DIM // 4
    inv_freq = ROPE_BASE ** (-(in_head % n_freq).astype(F32) / n_freq)
    pos = jnp.where(in_head < ATT_HEAD_DIM // 2, t // GRID_W, t % GRID_W).astype(F32)
    ang = pos * inv_freq
    sign = jnp.where((in_head % (2 * n_freq)) < n_freq, -1.0, 1.0)
    return jnp.cos(ang), sign * jnp.sin(ang)


def _tile_rows(seq):
    return min(seq, 512)


def kernel(x, c, ctx, c_ctx, w_mod, b_mod, norm_g, w_in, mlstm_qk_conv, mlstm_gate_bias, pool_w, pool_scale,
           attn_sink, sconv_w, w_branch, w_out, ffn_up, ffn_conv, ffn_down):
    bsz, seq, d = x.shape
    lc = ctx.shape[1]
    depth = w_mod.shape[0]
    assert d == D_MODEL and seq % 512 == 0 and lc % 256 == 0 and bsz + 1 <= 8

    cond = jnp.concatenate([c, c_ctx[None, :], jnp.zeros((8 - bsz - 1, d), F32)], axis=0)
    mod = _modulation(cond, w_mod, b_mod).reshape(depth, 8, 6, d)

    g0 = BRANCH_W * 4
    g1 = g0 + 4 * MLSTM_HEADS
    mg0 = w_in.shape[-1] - N_BRANCH * d
    p0 = g1
    aq0 = p0 + BRANCH_W
    ak0 = aq0 + BRANCH_W
    av0 = ak0 + ATT_KV_HEADS * ATT_HEAD_DIM
    sc0 = av0 + ATT_KV_HEADS * ATT_HEAD_DIM
    w_aq = w_in[:, :, aq0:ak0].reshape(depth, d, ATT_KV_HEADS, ATT_Q_HEADS // ATT_KV_HEADS, ATT_HEAD_DIM)
    w_aq = jnp.swapaxes(w_aq, 2, 3).reshape(depth, d, BRANCH_W)
    v0 = 2 * BRANCH_W
    w1 = jnp.concatenate([w_in[:, :, :v0], w_in[:, :, p0:aq0], w_aq, w_in[:, :, ak0:av0], w_in[:, :, sc0:mg0]],
                         axis=-1).astype(BF16)
    perm = jnp.array(_GATE_PERM)
    wt = jnp.concatenate([w_in[:, :, v0:g0], w_in[:, :, av0:sc0], w_in[:, :, g0:g1][:, :, perm]], axis=-1)
    wt = jnp.swapaxes(wt, 1, 2).astype(BF16)
    gb = mlstm_gate_bias.reshape(depth, 4 * MLSTM_HEADS)[:, perm][:, :, None]
    w_mg = w_in[:, :, mg0:].astype(BF16)
    pw = pool_w.astype(BF16)
    ps = pool_scale[:, None, :]
    sink_heads = attn_sink.reshape(depth, ATT_KV_HEADS, ATT_Q_HEADS // ATT_KV_HEADS).swapaxes(1, 2)
    sink = jnp.broadcast_to(sink_heads.reshape(depth, 1, ATT_Q_HEADS, 1),
                            (depth, 1, ATT_Q_HEADS, ATT_BLOCK)).reshape(depth, 1, ATT_Q_HEADS * ATT_BLOCK)
    wb = w_branch.astype(BF16)
    wo = w_out.astype(BF16)
    wu = ffn_up.astype(BF16)
    wc = ffn_conv
    wd = ffn_down.astype(BF16)

    cos_x, sin_x = _rope_tables(seq)
    cos_c, sin_c = jnp.ones((lc, LANES), F32), jnp.zeros((lc, LANES), F32)

    hd = MLSTM_HD
    zero_c = jnp.zeros((bsz, MLSTM_HEADS, 2 * hd, hd), F32)
    zero_m = jnp.zeros((bsz, 8, LANES), F32)
    tm_x, tm_c = _tile_rows(seq), _tile_rows(lc)
    scan = functools.partial(_mlstm, chunk=MLSTM_CHUNK)

    h = ctx
    for l in range(depth):
        ctx_out = l < depth - 1
        proj = functools.partial(_inproj, mod=mod, norm_g=norm_g, w1=w1, wt=wt, gb=gb, qkc=mlstm_qk_conv,
                                 scw=sconv_w, pw=pw, ps=ps, layer=l)
        qk_c, v_c, og_c, gt_c, yb_c, aq_c, ak_c, avt_c, yd_c = proj(h, cos=cos_c, sin=sin_c, mod_row=bsz, tm=tm_c)
        qk_x, v_x, og_x, gt_x, yb_x, aq_x, ak_x, avt_x, yd_x = proj(x, cos=cos_x, sin=sin_x, mod_row=None, tm=tm_x)
        hf_c, cf, mf = scan(qk_c, v_c, gt_c, zero_c, zero_m, reverse=False)
        hb_c, cb, mb = scan(qk_c, v_c, gt_c, zero_c, zero_m, reverse=True)
        hf_x, _, _ = scan(qk_x, v_x, gt_x, cf, mf, reverse=False)
        hb_x, _, _ = scan(qk_x, v_x, gt_x, cb, mb, reverse=True)
        att_x = _attention(aq_x, ak_x, avt_x, ak_c, avt_c, sink, layer=l, local=True)
        tail = dict(norm_g=norm_g, layer=l)
        x = _merge(x, mod, hf=hf_x, hb=hb_x, og=og_x, yb=yb_x, att=att_x, yd=yd_x, w_mg=w_mg, w_branch=wb,
                   w_out=wo, mod_row=None, tm=tm_x, **tail)
        x = _ffn(x, mod, w_up=wu, w_conv=wc, w_down=wd, mod_row=None, tm=tm_x, **tail)
        if ctx_out:
            att_c = _attention(aq_c, ak_c, avt_c, ak_c, avt_c, sink, layer=l, local=False)
            h = _merge(h, mod, hf=hf_c, hb=hb_c, og=og_c, yb=yb_c, att=att_c, yd=yd_c, w_mg=w_mg, w_branch=wb,
                       w_out=wo, mod_row=bsz, tm=tm_c, **tail)
            h = _ffn(h, mod, w_up=wu, w_conv=wc, w_down=wd, mod_row=bsz, tm=tm_c, **tail)
    return x
```

```python
import functools

import jax
import jax.numpy as jnp
from jax import lax
from jax.experimental import pallas as pl
from jax.experimental.pallas import tpu as pltpu

F32 = jnp.float32
BF16 = jnp.bfloat16
HIGHEST = lax.Precision.HIGHEST

D_MODEL = 1024
GRID_W = 64
N_BRANCH = 4
BRANCH_W = D_MODEL // 2
MLSTM_HEADS = 4
MLSTM_HD = BRANCH_W // MLSTM_HEADS
MLSTM_CHUNK = 256
MLSTM_CHUNKS_PER_STEP = 4
POOL_HALF = (1, 2, 4, 8)
POOL_GW = BRANCH_W // len(POOL_HALF)
ATT_HEAD_DIM = 64
ATT_Q_HEADS = BRANCH_W // ATT_HEAD_DIM
ATT_KV_HEADS = 2
ATT_BLOCK = 128
ATT_SUB = 4
ROPE_BASE = 10000.0
D_FF = 2816
EPS = 1e-6

LANES = 128
HALO = 8
FF_CHUNK = 256
VMEM_LIMIT = 56 * 1024 * 1024

_OFF = {}
_acc = 0
for _name, _w in (("q", 512), ("k", 512), ("pool", 512), ("aq", 512), ("ak", 128), ("sb", 512), ("sc", 512),
                  ("sx", 512)):
    _OFF[_name] = _acc
    _acc += _w
W1_COLS = _acc
_TOFF = {}
_acc = 0
for _name, _w in (("v", 512), ("o", 512), ("av", 128), ("gates", 16)):
    _TOFF[_name] = _acc
    _acc += _w
WT_ROWS = _acc
_GATE_PERM = (0, 1, 2, 3, 8, 9, 10, 11, 4, 5, 6, 7, 12, 13, 14, 15)


def _iota(shape, dim):
    return lax.broadcasted_iota(jnp.int32, shape, dim)


def _rms(x):
    return x * lax.rsqrt(jnp.mean(x * x, axis=-1, keepdims=True) + EPS)


def _sigmoid(x):
    return 1.0 / (1.0 + jnp.exp(-x))


def _nt(a, b, **kw):
    return lax.dot_general(a, b, (((1,), (1,)), ((), ())), preferred_element_type=F32, **kw)


def _tn(a, b, **kw):
    return lax.dot_general(a, b, (((0,), (0,)), ((), ())), preferred_element_type=F32, **kw)


def _mm(a, b, **kw):
    return jnp.dot(a, b, preferred_element_type=F32, **kw)


def _params(*sem, flags=None):
    return pltpu.CompilerParams(dimension_semantics=sem, vmem_limit_bytes=VMEM_LIMIT, flags=flags)


def _const_spec(shape, index_map):
    return pl.BlockSpec(shape, index_map, pipeline_mode=pl.Buffered(1))


def _mod_kernel(c_ref, w_ref, b_ref, o_ref):
    cond = c_ref[...]
    sc = cond * _sigmoid(cond)
    o_ref[0] = _mm(sc, w_ref[0], precision=HIGHEST) + b_ref[0]


def _modulation(cond, w_mod, b_mod):
    depth, d, n = w_mod.shape
    rows = cond.shape[0]
    tn = 1024
    return pl.pallas_call(
        _mod_kernel,
        grid=(depth, n // tn),
        in_specs=[pl.BlockSpec((rows, d), lambda l, j: (0, 0)),
                  pl.BlockSpec((1, d, tn), lambda l, j: (l, 0, j)),
                  pl.BlockSpec((1, 1, tn), lambda l, j: (l, 0, j))],
        out_specs=pl.BlockSpec((1, rows, tn), lambda l, j: (l, 0, j)),
        out_shape=jax.ShapeDtypeStruct((depth, rows, n), F32),
        compiler_params=_params("parallel", "parallel"),
        name="modulation",
    )(cond, w_mod, b_mod.reshape(depth, 1, n))


def _halo_specs(tm, d, seq):
    nh = seq // HALO
    per = tm // HALO
    main = pl.BlockSpec((1, tm, d), lambda b, i: (b, i, 0))
    prev = pl.BlockSpec((1, HALO, d), lambda b, i: (b, jnp.maximum(i * per - 1, 0), 0))
    nxt = pl.BlockSpec((1, HALO, d), lambda b, i: (b, jnp.minimum((i + 1) * per, nh - 1), 0))
    return main, prev, nxt


def _modulated_tile(xm_ref, xp_ref, xn_ref, gain, shift, scale, tm, seq):
    i = pl.program_id(1)
    xx = jnp.concatenate([xp_ref[0], xm_ref[0], xn_ref[0]], axis=0)
    u = _rms(xx) * gain * (1.0 + scale) + shift
    pos = i * tm - HALO + _iota((tm + 2 * HALO, 1), 0)
    return jnp.where((pos >= 0) & (pos < seq), u, 0.0)


def _conv3_taps(r_ref, w, tm):
    return (r_ref[HALO - 1:HALO - 1 + tm, :] * w[0:1] + r_ref[HALO:HALO + tm, :] * w[1:2]
            + r_ref[HALO + 1:HALO + 1 + tm, :] * w[2:3])


def _conv3(r_ref, w_ref, cols, tm):
    return _conv3_taps(r_ref, w_ref[0, :, cols], tm)


def _rope(r, cos, sin):
    lane = _iota((1, LANES), 1)
    first = (lane % 32) < 16
    swapped = jnp.where(first, pltpu.roll(r, LANES - 16, 1), pltpu.roll(r, 16, 1))
    return r * cos + swapped * sin


def _inproj_kernel(xm_ref, xp_ref, xn_ref, mod_ref, g_ref, w_ref, wt_ref, gb_ref, qkc_ref, scw_ref, pw_ref,
                   ps_ref, cos_ref, sin_ref,
                   qk_out, vt_out, ogt_out, gates_out, yb_out, aq_out, ak_out, avt_out, yd_out,
                   u_scr, um_scr, r_scr, *, tm, seq):
    i = pl.program_id(1)
    u = _modulated_tile(xm_ref, xp_ref, xn_ref, g_ref[0, 0:1, :], mod_ref[0, 0, 0:1, :], mod_ref[0, 0, 1:2, :],
                        tm, seq)
    u_scr[...] = u.astype(BF16)
    um_scr[...] = u[HALO:HALO + tm].astype(BF16)

    def wcols(name, width, extra=0):
        o = _OFF[name] + extra
        return w_ref[0, :, o:o + width]

    for c, name in enumerate(("q", "k")):
        cols = slice(c * BRANCH_W, (c + 1) * BRANCH_W)
        r_scr[...] = _mm(u_scr[...], wcols(name, BRANCH_W))
        half = _conv3_taps(r_scr, 0.5 * qkc_ref[0, :, cols], tm)
        a = half + half * jnp.tanh(half)
        if name == "k":
            a = a * MLSTM_HD ** -0.5
        qk_out[0, :, cols] = a.astype(BF16)

    um = um_scr[...]

    def trows(name, height):
        o = _TOFF[name]
        return wt_ref[0, o:o + height, :]

    vt_out[0] = _nt(trows("v", BRANCH_W), um).astype(BF16)
    ogt_out[0] = _sigmoid(_nt(trows("o", BRANCH_W), um)).astype(BF16)
    avt_out[0] = _nt(trows("av", LANES), um).astype(BF16)

    g = _nt(trows("gates", 16), um) + gb_ref[0]
    logsig = jnp.minimum(g, 0.0) - jnp.log(1.0 + jnp.exp(-jnp.abs(g)))
    is_forget = (_iota((16, 1), 0) % 8) >= MLSTM_HEADS
    g = jnp.where(is_forget, logsig, g)
    gates_out[0, 0] = g[0:8]
    gates_out[0, 1] = g[8:16]

    r_scr[...] = _mm(u_scr[...], wcols("pool", BRANCH_W))
    t = i * tm + _iota((tm, 1), 0)
    for gi, half in enumerate(POOL_HALF):
        cols = slice(gi * POOL_GW, (gi + 1) * POOL_GW)
        acc = r_scr[HALO - half:HALO - half + tm, cols]
        for o in range(-half + 1, half):
            acc = acc + r_scr[HALO + o:HALO + o + tm, cols]
        cnt = (jnp.minimum(t + half, seq) - jnp.maximum(t - half, 0)).astype(F32)
        delta = acc / cnt - r_scr[HALO:HALO + tm, cols]
        y = _mm(delta.astype(BF16), pw_ref[0, gi]) * ps_ref[0, 0:1, cols]
        yb_out[0, :, cols] = y.astype(BF16)

    cos = cos_ref[...]
    sin = sin_ref[...]
    rq = _mm(um, wcols("aq", BRANCH_W))
    for j in range(BRANCH_W // LANES):
        cols = slice(j * LANES, (j + 1) * LANES)
        aq_out[0, :, cols] = (_rope(rq[:, cols], cos, sin) * ATT_HEAD_DIM ** -0.5).astype(BF16)
    ak_out[0] = _rope(_mm(um, wcols("ak", LANES)), cos, sin).astype(BF16)

    rb = _mm(um, wcols("sb", BRANCH_W))
    r_scr[...] = _mm(u_scr[...], wcols("sc", BRANCH_W)) * _mm(u_scr[...], wcols("sx", BRANCH_W))
    yd_out[0] = (rb * _conv3(r_scr, scw_ref, slice(0, BRANCH_W), tm)).astype(BF16)


def _inproj(x, mod, norm_g, w1, wt, gb, qkc, scw, pw, ps, cos, sin, *, layer, mod_row, tm):
    bsz, seq, d = x.shape
    nt = seq // tm
    main, prev, nxt = _halo_specs(tm, d, seq)
    mrow = (lambda b: b) if mod_row is None else (lambda b: mod_row)
    lsel3 = lambda b, i: (layer, 0, 0)
    in_specs = [
        main, prev, nxt,
        pl.BlockSpec((1, 1, 6, d), lambda b, i: (layer, mrow(b), 0, 0)),
        _const_spec((1, 4, d), lsel3),
        _const_spec((1, d, W1_COLS), lsel3),
        _const_spec((1, WT_ROWS, d), lsel3),
        _const_spec((1, 16, 1), lsel3),
        _const_spec((1, 3, 2 * BRANCH_W), lsel3),
        _const_spec((1, 3, BRANCH_W), lsel3),
        _const_spec((1, len(POOL_HALF), POOL_GW, POOL_GW), lambda b, i: (layer, 0, 0, 0)),
        _const_spec((1, 1, BRANCH_W), lsel3),
        pl.BlockSpec((tm, LANES), lambda b, i: (i, 0)),
        pl.BlockSpec((tm, LANES), lambda b, i: (i, 0)),
    ]
    tok = lambda w: pl.BlockSpec((1, tm, w), lambda b, i: (b, i, 0))
    feat = lambda w: pl.BlockSpec((1, w, tm), lambda b, i: (b, 0, i))
    out_specs = [tok(2 * BRANCH_W), feat(BRANCH_W), feat(BRANCH_W),
                 pl.BlockSpec((1, 2, 8, tm), lambda b, i: (b, 0, 0, i)),
                 tok(BRANCH_W), tok(BRANCH_W), tok(LANES), feat(LANES), tok(BRANCH_W)]
    act = lambda w: jax.ShapeDtypeStruct((bsz, seq, w), BF16)
    act_t = lambda w: jax.ShapeDtypeStruct((bsz, w, seq), BF16)
    out_shape = [act(2 * BRANCH_W), act_t(BRANCH_W), act_t(BRANCH_W),
                 jax.ShapeDtypeStruct((bsz, 2, 8, seq), F32),
                 act(BRANCH_W), act(BRANCH_W), act(LANES), act_t(LANES), act(BRANCH_W)]
    return pl.pallas_call(
        functools.partial(_inproj_kernel, tm=tm, seq=seq),
        grid=(bsz, nt),
        in_specs=in_specs,
        out_specs=out_specs,
        out_shape=out_shape,
        scratch_shapes=[pltpu.VMEM((tm + 2 * HALO, d), BF16), pltpu.VMEM((tm, d), BF16),
                        pltpu.VMEM((tm + 2 * HALO, BRANCH_W), F32)],
        compiler_params=_params("parallel", "parallel"),
        name="inproj",
    )(x, x, x, mod, norm_g, w1, wt, gb, qkc, scw, pw, ps, cos, sin)


def _split3(x):
    x = jnp.concatenate([x, jnp.zeros_like(x)], axis=0)
    hi = x.astype(BF16)
    r1 = x - hi.astype(F32)
    mid = r1.astype(BF16)
    lo = (r1 - mid.astype(F32)).astype(BF16)
    return jnp.concatenate([hi, mid, lo], axis=0)


def _sum3(y):
    return (y[0:8] + y[16:24]) + y[32:40]


def _mlstm_kernel(qk_ref, vt_ref, g_ref, c0_ref, m0_ref, h_out, c_out, m_out, *, chunk, reverse):
    L = chunk
    hd = MLSTM_HD
    nck = qk_ref.shape[1] // L

    @pl.when(pl.program_id(1) == 0)
    def _():
        c_out[...] = c0_ref[...]
        m_out[...] = m0_ref[...]

    s_idx = _iota((L, L), 0)
    t_idx = _iota((L, L), 1)
    seen = (s_idx >= t_idx) if reverse else (s_idx <= t_idx)
    seen_t = (s_idx <= t_idx) if reverse else (s_idx >= t_idx)
    bias = jnp.where(seen, 0.0, -jnp.inf)
    seen_bf = seen.astype(BF16)
    seen_t_bf = seen_t.astype(BF16)
    eye = (s_idx == t_idx).astype(BF16)
    is_last = _iota((1, L), 1) == (0 if reverse else L - 1)
    ones = jnp.ones((hd, L), BF16)
    neg_inf = -jnp.inf

    for i in range(nck):
        sub = (nck - 1 - i) if reverse else i
        tok = slice(sub * L, (sub + 1) * L)
        g = g_ref[0, 0, :, tok]
        g3 = _split3(g)
        cum_row = _sum3(_mm(g3, seen_bf))
        col3 = [_nt(seen_t_bf, g3[16 * j:16 * (j + 1)]) for j in range(3)]
        gcol3 = [_nt(eye, g3[16 * j:16 * (j + 1)]) for j in range(3)]
        cum_col = (col3[0] + col3[1]) + col3[2]
        g_col = (gcol3[0] + gcol3[1]) + gcol3[2]

        heads = range(MLSTM_HEADS)
        qs = [qk_ref[0, tok, h * hd:(h + 1) * hd] for h in heads]
        ks = [qk_ref[0, tok, BRANCH_W + h * hd:BRANCH_W + (h + 1) * hd] for h in heads]
        vaugs = [jnp.concatenate([vt_ref[0, h * hd:(h + 1) * hd, tok], ones], axis=0) for h in heads]
        wide = lambda parts: jnp.concatenate(parts, axis=1)
        b_row = wide([cum_row[MLSTM_HEADS + h:MLSTM_HEADS + h + 1, :] for h in heads])
        ig_row = wide([g[h:h + 1, :] for h in heads])
        key = wide([jnp.broadcast_to(cum_col[:, MLSTM_HEADS + h:MLSTM_HEADS + h + 1] - g_col[:, h:h + 1], (L, L))
                    for h in heads])
        m_prev = wide([jnp.broadcast_to(m_out[0, h:h + 1, 0:1], (1, L)) for h in heads])
        dmat = (b_row - key) + wide([bias] * MLSTM_HEADS)
        a = b_row + m_prev
        mt = jnp.maximum(a, jnp.max(dmat, axis=0, keepdims=True))
        p = (wide([_nt(ks[h], qs[h]) for h in heads]) * jnp.exp(dmat - mt)).astype(BF16)
        inter = jnp.exp(a - mt)
        floor = jnp.exp(-mt)
        last = wide([is_last] * MLSTM_HEADS)
        mt_last = jnp.where(last, mt, neg_inf)
        b_last = jnp.where(last, b_row, neg_inf)
        for h in heads:
            rows = slice(h * hd, (h + 1) * hd)
            lanes = slice(h * L, (h + 1) * L)
            c_prev = c_out[0, h]
            r = _mm(vaugs[h], p[:, lanes]) + inter[:, lanes] * _nt(c_prev.astype(BF16), qs[h])
            den = jnp.maximum(jnp.abs(r[hd:]), floor[:, lanes])
            h_out[0, rows, tok] = (r[:hd] / den).astype(BF16)
            m_new = jnp.max(mt_last[:, lanes], axis=1, keepdims=True)
            b_tot = jnp.max(b_last[:, lanes], axis=1, keepdims=True)
            decay = jnp.exp(b_tot + m_prev[:, h * L:h * L + 1] - m_new)
            wk = jnp.exp(b_tot - b_row[:, lanes] + ig_row[:, lanes] - m_new)
            c_out[0, h] = decay * c_prev + _mm((vaugs[h].astype(F32) * wk).astype(BF16), ks[h])
            m_out[0, h:h + 1, :] = jnp.broadcast_to(m_new, (1, LANES))


def _mlstm(qk, vt, gates, c0, m0, *, chunk, reverse):
    bsz, seq, _ = qk.shape
    nck = min(MLSTM_CHUNKS_PER_STEP, seq // chunk)
    tb = nck * chunk
    ns = seq // tb
    hd = MLSTM_HD
    direction = 1 if reverse else 0
    blk = (lambda j: ns - 1 - j) if reverse else (lambda j: j)
    state_c = pl.BlockSpec((1, MLSTM_HEADS, 2 * hd, hd), lambda b, j: (b, 0, 0, 0))
    state_m = pl.BlockSpec((1, 8, LANES), lambda b, j: (b, 0, 0))
    return pl.pallas_call(
        functools.partial(_mlstm_kernel, chunk=chunk, reverse=reverse),
        grid=(bsz, ns),
        in_specs=[pl.BlockSpec((1, tb, 2 * BRANCH_W), lambda b, j: (b, blk(j), 0)),
                  pl.BlockSpec((1, BRANCH_W, tb), lambda b, j: (b, 0, blk(j))),
                  pl.BlockSpec((1, 1, 8, tb), lambda b, j: (b, direction, 0, blk(j))),
                  state_c, state_m],
        out_specs=[pl.BlockSpec((1, BRANCH_W, tb), lambda b, j: (b, 0, blk(j))),
                   state_c, state_m],
        out_shape=[jax.ShapeDtypeStruct((bsz, BRANCH_W, seq), BF16),
                   jax.ShapeDtypeStruct((bsz, MLSTM_HEADS, 2 * hd, hd), F32),
                   jax.ShapeDtypeStruct((bsz, 8, LANES), F32)],
        compiler_params=_params("parallel", "arbitrary"),
        name="mlstm_bwd" if reverse else "mlstm_fwd",
    )(qk, vt, gates, c0, m0)


def _attn_kernel(*refs, nb, local):
    blk = ATT_BLOCK
    hd = ATT_HEAD_DIM
    if local:
        q_ref, kp_ref, kc_ref, kn_ref, kx_ref, vp_ref, vc_ref, vn_ref, vx_ref, sink_ref, o_ref = refs
        k_loc = jnp.concatenate([kp_ref[0], kc_ref[0], kn_ref[0]], axis=0)
        vt_loc = jnp.concatenate([vp_ref[0], vc_ref[0], vn_ref[0]], axis=1)
    else:
        q_ref, kx_ref, vx_ref, sink_ref, o_ref = refs
    nsub = q_ref.shape[1] // blk
    step = pl.program_id(1)
    lo = _iota((1, LANES), 1) < hd
    nq = ATT_Q_HEADS * blk
    sk = sink_ref[0]
    if local:
        kj = _iota((blk, nq), 0)
        qi = _iota((blk, nq), 1) % blk
        after = kj >= qi
        before = kj <= qi

    scores, values, prev_bias, next_bias = [], [], [], []
    for i in range(nsub):
        if local:
            k = jnp.concatenate([k_loc[i * blk:(i + 3) * blk], kx_ref[0]], axis=0)
            vt = jnp.concatenate([vt_loc[:, i * blk:(i + 3) * blk], vx_ref[0]], axis=1)
            n = step * nsub + i
            prev_bias.append(jnp.where(after & (n > 0), 0.0, -jnp.inf))
            next_bias.append(jnp.where(before & (n < nb - 1), 0.0, -jnp.inf))
        else:
            k = kx_ref[0]
            vt = vx_ref[0]
        nk = k.shape[0]
        values.append(jnp.concatenate([vt, jnp.ones((16, nk), BF16)], axis=0))
        parts = []
        for j in range(BRANCH_W // LANES):
            q2 = q_ref[0, i * blk:(i + 1) * blk, j * LANES:(j + 1) * LANES]
            zero = jnp.zeros_like(q2)
            parts += [jnp.where(lo, q2, zero), jnp.where(lo, zero, q2)]
        scores.append(_nt(k, jnp.concatenate(parts, axis=0)))
    s = jnp.concatenate(scores, axis=1)
    if local:
        s = jnp.concatenate([s[0:blk] + jnp.concatenate(prev_bias, axis=1), s[blk:2 * blk],
                             s[2 * blk:3 * blk] + jnp.concatenate(next_bias, axis=1), s[3 * blk:]], axis=0)
    sk = jnp.concatenate([sk] * nsub, axis=1)
    m = jnp.maximum(jnp.max(s, axis=0, keepdims=True), sk)
    e = jnp.exp(s - m).astype(BF16)
    extra = jnp.exp(sk - m)
    for i in range(nsub):
        r = _mm(values[i], e[:, i * nq:(i + 1) * nq])
        den = r[2 * hd:2 * hd + 1, :] + extra[:, i * nq:(i + 1) * nq]
        for p in range(ATT_Q_HEADS):
            c = p % 2
            head = p // 2 + 4 * c
            cols = slice(p * blk, (p + 1) * blk)
            o_ref[0, head * hd:(head + 1) * hd, i * blk:(i + 1) * blk] = (
                r[c * hd:(c + 1) * hd, cols] / den[:, cols]).astype(BF16)


def _attention(aq, ak, avt, ak_ctx, avt_ctx, sink, *, layer, local):
    bsz, seq, _ = aq.shape
    nb = seq // ATT_BLOCK
    nsub = min(ATT_SUB, nb)
    tq = nsub * ATT_BLOCK
    lc = ak_ctx.shape[1]
    qspec = pl.BlockSpec((1, tq, BRANCH_W), lambda b, n: (b, n, 0))
    kx_spec = pl.BlockSpec((1, lc, LANES), lambda b, n: (b, 0, 0))
    vx_spec = pl.BlockSpec((1, LANES, lc), lambda b, n: (b, 0, 0))
    sink_spec = pl.BlockSpec((1, 1, ATT_Q_HEADS * ATT_BLOCK), lambda b, n: (layer, 0, 0))
    if local:
        prev = lambda n: jnp.maximum(n * nsub - 1, 0)
        nxt = lambda n: jnp.minimum((n + 1) * nsub, nb - 1)
        kspecs = [pl.BlockSpec((1, ATT_BLOCK, LANES), lambda b, n: (b, prev(n), 0)),
                  pl.BlockSpec((1, tq, LANES), lambda b, n: (b, n, 0)),
                  pl.BlockSpec((1, ATT_BLOCK, LANES), lambda b, n: (b, nxt(n), 0))]
        vspecs = [pl.BlockSpec((1, LANES, ATT_BLOCK), lambda b, n: (b, 0, prev(n))),
                  pl.BlockSpec((1, LANES, tq), lambda b, n: (b, 0, n)),
                  pl.BlockSpec((1, LANES, ATT_BLOCK), lambda b, n: (b, 0, nxt(n)))]
        in_specs = [qspec, *kspecs, kx_spec, *vspecs, vx_spec, sink_spec]
        args = (aq, ak, ak, ak, ak_ctx, avt, avt, avt, avt_ctx, sink)
    else:
        in_specs = [qspec, kx_spec, vx_spec, sink_spec]
        args = (aq, ak_ctx, avt_ctx, sink)
    return pl.pallas_call(
        functools.partial(_attn_kernel, nb=nb, local=local),
        grid=(bsz, nb // nsub),
        in_specs=in_specs,
        out_specs=pl.BlockSpec((1, BRANCH_W, tq), lambda b, n: (b, 0, n)),
        out_shape=jax.ShapeDtypeStruct((bsz, BRANCH_W, seq), BF16),
        compiler_params=_params("parallel", "parallel"),
        name="attention",
    )(*args)


def _merge_kernel(x_ref, mod_ref, g_ref, hf_ref, hb_ref, og_ref, yb_ref, att_ref, yd_ref, wg_ref, wb_ref, wo_ref,
                  o_ref):
    d = D_MODEL
    x = x_ref[0]
    u = (_rms(x) * g_ref[0, 0:1, :] * (1.0 + mod_ref[0, 0, 1:2, :]) + mod_ref[0, 0, 0:1, :]).astype(BF16)
    ya_t = (og_ref[0].astype(F32) * (hf_ref[0].astype(F32) + hb_ref[0].astype(F32))).astype(BF16)
    ys = (ya_t, yb_ref[0], att_ref[0], yd_ref[0])
    feature_major = (True, False, True, False)
    merged = None
    for n in range(N_BRANCH):
        gate = _sigmoid(_mm(u, wg_ref[0, :, n * d:(n + 1) * d]))
        proj = _tn(ys[n], wb_ref[0, n]) if feature_major[n] else _mm(ys[n], wb_ref[0, n])
        term = gate * proj
        merged = term if merged is None else merged + term
    out = _mm(merged.astype(BF16), wo_ref[0])
    o_ref[0] = x + mod_ref[0, 0, 2:3, :] * (_rms(out) * g_ref[0, 1:2, :])


def _merge(x, mod, norm_g, hf, hb, og, yb, att, yd, w_mg, w_branch, w_out, *, layer, mod_row, tm):
    bsz, seq, d = x.shape
    mrow = (lambda b: b) if mod_row is None else (lambda b: mod_row)
    tok = lambda w: pl.BlockSpec((1, tm, w), lambda b, i: (b, i, 0))
    feat = pl.BlockSpec((1, BRANCH_W, tm), lambda b, i: (b, 0, i))
    lsel3 = lambda b, i: (layer, 0, 0)
    return pl.pallas_call(
        _merge_kernel,
        grid=(bsz, seq // tm),
        in_specs=[tok(d),
                  pl.BlockSpec((1, 1, 6, d), lambda b, i: (layer, mrow(b), 0, 0)),
                  _const_spec((1, 4, d), lsel3),
                  feat, feat, feat, tok(BRANCH_W), feat, tok(BRANCH_W),
                  _const_spec((1, d, N_BRANCH * d), lsel3),
                  _const_spec((1, N_BRANCH, BRANCH_W, d), lambda b, i: (layer, 0, 0, 0)),
                  _const_spec((1, d, d), lsel3)],
        out_specs=tok(d),
        out_shape=jax.ShapeDtypeStruct((bsz, seq, d), F32),
        compiler_params=_params("parallel", "parallel"),
        name="merge",
    )(x, mod, norm_g, hf, hb, og, yb, att, yd, w_mg, w_branch, w_out)


def _ffn_kernel(xm_ref, xp_ref, xn_ref, mod_ref, g_ref, wu_ref, wc_ref, wd_ref, o_ref, u_scr, rg_scr, rv_scr,
                h_scr, *, tm, seq):
    u = _modulated_tile(xm_ref, xp_ref, xn_ref, g_ref[0, 2:3, :], mod_ref[0, 0, 3:4, :], mod_ref[0, 0, 4:5, :],
                        tm, seq)
    u_scr[...] = u.astype(BF16)
    cw = FF_CHUNK
    for c in range(D_FF // cw):
        gcols = slice(c * cw, (c + 1) * cw)
        vcols = slice(D_FF + c * cw, D_FF + (c + 1) * cw)
        rg_scr[...] = _mm(u_scr[...], wu_ref[0, :, gcols])
        rv_scr[...] = _mm(u_scr[...], wu_ref[0, :, vcols])
        half = _conv3_taps(rg_scr, 0.5 * wc_ref[0, :, gcols], tm)
        val = _conv3(rv_scr, wc_ref, vcols, tm)
        h_scr[:, gcols] = ((half + half * jnp.tanh(half)) * val).astype(BF16)
    out = _mm(h_scr[...], wd_ref[0])
    x = xm_ref[0]
    o_ref[0] = x + mod_ref[0, 0, 5:6, :] * (_rms(out) * g_ref[0, 3:4, :])


def _ffn(x, mod, norm_g, w_up, w_conv, w_down, *, layer, mod_row, tm):
    bsz, seq, d = x.shape
    main, prev, nxt = _halo_specs(tm, d, seq)
    mrow = (lambda b: b) if mod_row is None else (lambda b: mod_row)
    lsel3 = lambda b, i: (layer, 0, 0)
    return pl.pallas_call(
        functools.partial(_ffn_kernel, tm=tm, seq=seq),
        grid=(bsz, seq // tm),
        in_specs=[main, prev, nxt,
                  pl.BlockSpec((1, 1, 6, d), lambda b, i: (layer, mrow(b), 0, 0)),
                  _const_spec((1, 4, d), lsel3),
                  _const_spec((1, d, 2 * D_FF), lsel3),
                  _const_spec((1, 3, 2 * D_FF), lsel3),
                  _const_spec((1, D_FF, d), lsel3)],
        out_specs=main,
        out_shape=jax.ShapeDtypeStruct((bsz, seq, d), F32),
        scratch_shapes=[pltpu.VMEM((tm + 2 * HALO, d), BF16), pltpu.VMEM((tm + 2 * HALO, FF_CHUNK), F32),
                        pltpu.VMEM((tm + 2 * HALO, FF_CHUNK), F32), pltpu.VMEM((tm, D_FF), BF16)],
        compiler_params=_params("parallel", "parallel"),
        name="convffn",
    )(x, x, x, mod, norm_g, w_up, w_conv, w_down)


def _rope_tables(seq):
    t = jnp.arange(seq, dtype=jnp.int32)[:, None]
    lane = jnp.arange(LANES, dtype=jnp.int32)[None, :]
    in_head = lane % ATT_HEAD_DIM
    n_freq = ATT_HEAD_DIM // 4
    inv_freq = ROPE_BASE ** (-(in_head % n_freq).astype(F32) / n_freq)
    pos = jnp.where(in_head < ATT_HEAD_DIM // 2, t // GRID_W, t % GRID_W).astype(F32)
    ang = pos * inv_freq
    sign = jnp.where((in_head % (2 * n_freq)) < n_freq, -1.0, 1.0)
    return jnp.cos(ang), sign * jnp.sin(ang)


def _tile_rows(seq):
    return min(seq, 512)


def kernel(x, c, ctx, c_ctx, w_mod, b_mod, norm_g, w_in, mlstm_qk_conv, mlstm_gate_bias, pool_w, pool_scale,
           attn_sink, sconv_w, w_branch, w_out, ffn_up, ffn_conv, ffn_down):
    bsz, seq, d = x.shape
    lc = ctx.shape[1]
    depth = w_mod.shape[0]
    assert d == D_MODEL and seq % 512 == 0 and lc % 256 == 0 and bsz + 1 <= 8

    cond = jnp.concatenate([c, c_ctx[None, :], jnp.zeros((8 - bsz - 1, d), F32)], axis=0)
    mod = _modulation(cond, w_mod, b_mod).reshape(depth, 8, 6, d)

    g0 = BRANCH_W * 4
    g1 = g0 + 4 * MLSTM_HEADS
    mg0 = w_in.shape[-1] - N_BRANCH * d
    p0 = g1
    aq0 = p0 + BRANCH_W
    ak0 = aq0 + BRANCH_W
    av0 = ak0 + ATT_KV_HEADS * ATT_HEAD_DIM
    sc0 = av0 + ATT_KV_HEADS * ATT_HEAD_DIM
    w_aq = w_in[:, :, aq0:ak0].reshape(depth, d, ATT_KV_HEADS, ATT_Q_HEADS // ATT_KV_HEADS, ATT_HEAD_DIM)
    w_aq = jnp.swapaxes(w_aq, 2, 3).reshape(depth, d, BRANCH_W)
    v0 = 2 * BRANCH_W
    w1 = jnp.concatenate([w_in[:, :, :v0], w_in[:, :, p0:aq0], w_aq, w_in[:, :, ak0:av0], w_in[:, :, sc0:mg0]],
                         axis=-1).astype(BF16)
    perm = jnp.array(_GATE_PERM)
    wt = jnp.concatenate([w_in[:, :, v0:g0], w_in[:, :, av0:sc0], w_in[:, :, g0:g1][:, :, perm]], axis=-1)
    wt = jnp.swapaxes(wt, 1, 2).astype(BF16)
    gb = mlstm_gate_bias.reshape(depth, 4 * MLSTM_HEADS)[:, perm][:, :, None]
    w_mg = w_in[:, :, mg0:].astype(BF16)
    pw = pool_w.astype(BF16)
    ps = pool_scale[:, None, :]
    sink_heads = attn_sink.reshape(depth, ATT_KV_HEADS, ATT_Q_HEADS // ATT_KV_HEADS).swapaxes(1, 2)
    sink = jnp.broadcast_to(sink_heads.reshape(depth, 1, ATT_Q_HEADS, 1),
                            (depth, 1, ATT_Q_HEADS, ATT_BLOCK)).reshape(depth, 1, ATT_Q_HEADS * ATT_BLOCK)
    wb = w_branch.astype(BF16)
    wo = w_out.astype(BF16)
    wu = ffn_up.astype(BF16)
    wc = ffn_conv
    wd = ffn_down.astype(BF16)

    cos_x, sin_x = _rope_tables(seq)
    cos_c, sin_c = jnp.ones((lc, LANES), F32), jnp.zeros((lc, LANES), F32)

    hd = MLSTM_HD
    zero_c = jnp.zeros((bsz, MLSTM_HEADS, 2 * hd, hd), F32)
    zero_m = jnp.zeros((bsz, 8, LANES), F32)
    tm_x, tm_c = _tile_rows(seq), _tile_rows(lc)
    scan = functools.partial(_mlstm, chunk=MLSTM_CHUNK)

    h = ctx
    for l in range(depth):
        ctx_out = l < depth - 1
        proj = functools.partial(_inproj, mod=mod, norm_g=norm_g, w1=w1, wt=wt, gb=gb, qkc=mlstm_qk_conv,
                                 scw=sconv_w, pw=pw, ps=ps, layer=l)
        qk_c, v_c, og_c, gt_c, yb_c, aq_c, ak_c, avt_c, yd_c = proj(h, cos=cos_c, sin=sin_c, mod_row=bsz, tm=tm_c)
        qk_x, v_x, og_x, gt_x, yb_x, aq_x, ak_x, avt_x, yd_x = proj(x, cos=cos_x, sin=sin_x, mod_row=None, tm=tm_x)
        hf_c, cf, mf = scan(qk_c, v_c, gt_c, zero_c, zero_m, reverse=False)
        hb_c, cb, mb = scan(qk_c, v_c, gt_c, zero_c, zero_m, reverse=True)
        hf_x, _, _ = scan(qk_x, v_x, gt_x, cf, mf, reverse=False)
        hb_x, _, _ = scan(qk_x, v_x, gt_x, cb, mb, reverse=True)
        att_x = _attention(aq_x, ak_x, avt_x, ak_c, avt_c, sink, layer=l, local=True)
        tail = dict(norm_g=norm_g, layer=l)
        x = _merge(x, mod, hf=hf_x, hb=hb_x, og=og_x, yb=yb_x, att=att_x, yd=yd_x, w_mg=w_mg, w_branch=wb,
                   w_out=wo, mod_row=None, tm=tm_x, **tail)
        x = _ffn(x, mod, w_up=wu, w_conv=wc, w_down=wd, mod_row=None, tm=tm_x, **tail)
        if ctx_out:
            att_c = _attention(aq_c, ak_c, avt_c, ak_c, avt_c, sink, layer=l, local=False)
            h = _merge(h, mod, hf=hf_c, hb=hb_c, og=og_c, yb=yb_c, att=att_c, yd=yd_c, w_mg=w_mg, w_branch=wb,
                       w_out=wo, mod_row=bsz, tm=tm_c, **tail)
            h = _ffn(h, mod, w_up=wu, w_conv=wc, w_down=wd, mod_row=bsz, tm=tm_c, **tail)
    return x
```

```python
import functools

import jax
import jax.numpy as jnp
from jax import lax
from jax.experimental import pallas as pl
from jax.experimental.pallas import tpu as pltpu

F32 = jnp.float32
BF16 = jnp.bfloat16
HIGHEST = lax.Precision.HIGHEST

D_MODEL = 1024
GRID_W = 64
N_BRANCH = 4
BRANCH_W = D_MODEL // 2
MLSTM_HEADS = 4
MLSTM_HD = BRANCH_W // MLSTM_HEADS
MLSTM_CHUNK = 256
MLSTM_CHUNKS_PER_STEP = 8
POOL_HALF = (1, 2, 4, 8)
POOL_GW = BRANCH_W // len(POOL_HALF)
ATT_HEAD_DIM = 64
ATT_Q_HEADS = BRANCH_W // ATT_HEAD_DIM
ATT_KV_HEADS = 2
ATT_BLOCK = 128
ATT_SUB = 4
ROPE_BASE = 10000.0
D_FF = 2816
EPS = 1e-6

LANES = 128
HALO = 8
FF_CHUNK = 256
VMEM_LIMIT = 56 * 1024 * 1024

_OFF = {}
_acc = 0
for _name, _w in (("q", 512), ("k", 512), ("pool", 512), ("aq", 512), ("ak", 128), ("sb", 512), ("sc", 512),
                  ("sx", 512)):
    _OFF[_name] = _acc
    _acc += _w
W1_COLS = _acc
_TOFF = {}
_acc = 0
for _name, _w in (("v", 512), ("o", 512), ("av", 128), ("gates", 16)):
    _TOFF[_name] = _acc
    _acc += _w
WT_ROWS = _acc
_GATE_PERM = (0, 1, 2, 3, 8, 9, 10, 11, 4, 5, 6, 7, 12, 13, 14, 15)


def _iota(shape, dim):
    return lax.broadcasted_iota(jnp.int32, shape, dim)


def _rms(x):
    return x * lax.rsqrt(jnp.mean(x * x, axis=-1, keepdims=True) + EPS)


def _sigmoid(x):
    return 1.0 / (1.0 + jnp.exp(-x))


def _nt(a, b, **kw):
    return lax.dot_general(a, b, (((1,), (1,)), ((), ())), preferred_element_type=F32, **kw)


def _tn(a, b, **kw):
    return lax.dot_general(a, b, (((0,), (0,)), ((), ())), preferred_element_type=F32, **kw)


def _mm(a, b, **kw):
    return jnp.dot(a, b, preferred_element_type=F32, **kw)


def _params(*sem, flags=None):
    return pltpu.CompilerParams(dimension_semantics=sem, vmem_limit_bytes=VMEM_LIMIT, flags=flags)


def _const_spec(shape, index_map):
    return pl.BlockSpec(shape, index_map, pipeline_mode=pl.Buffered(1))


def _mod_kernel(c_ref, w_ref, b_ref, o_ref):
    cond = c_ref[...]
    sc = cond * _sigmoid(cond)
    o_ref[0] = _mm(sc, w_ref[0], precision=HIGHEST) + b_ref[0]


def _modulation(cond, w_mod, b_mod):
    depth, d, n = w_mod.shape
    rows = cond.shape[0]
    tn = 1024
    return pl.pallas_call(
        _mod_kernel,
        grid=(depth, n // tn),
        in_specs=[pl.BlockSpec((rows, d), lambda l, j: (0, 0)),
                  pl.BlockSpec((1, d, tn), lambda l, j: (l, 0, j)),
                  pl.BlockSpec((1, 1, tn), lambda l, j: (l, 0, j))],
        out_specs=pl.BlockSpec((1, rows, tn), lambda l, j: (l, 0, j)),
        out_shape=jax.ShapeDtypeStruct((depth, rows, n), F32),
        compiler_params=_params("parallel", "parallel"),
        name="modulation",
    )(cond, w_mod, b_mod.reshape(depth, 1, n))


def _halo_specs(tm, d, seq):
    nh = seq // HALO
    per = tm // HALO
    main = pl.BlockSpec((1, tm, d), lambda b, i: (b, i, 0))
    prev = pl.BlockSpec((1, HALO, d), lambda b, i: (b, jnp.maximum(i * per - 1, 0), 0))
    nxt = pl.BlockSpec((1, HALO, d), lambda b, i: (b, jnp.minimum((i + 1) * per, nh - 1), 0))
    return main, prev, nxt


def _modulated_tile(xm_ref, xp_ref, xn_ref, gain, shift, scale, tm, seq):
    i = pl.program_id(1)
    xx = jnp.concatenate([xp_ref[0], xm_ref[0], xn_ref[0]], axis=0)
    u = _rms(xx) * gain * (1.0 + scale) + shift
    pos = i * tm - HALO + _iota((tm + 2 * HALO, 1), 0)
    return jnp.where((pos >= 0) & (pos < seq), u, 0.0)


def _conv3_taps(r_ref, w, tm):
    return (r_ref[HALO - 1:HALO - 1 + tm, :] * w[0:1] + r_ref[HALO:HALO + tm, :] * w[1:2]
            + r_ref[HALO + 1:HALO + 1 + tm, :] * w[2:3])


def _conv3(r_ref, w_ref, cols, tm):
    return _conv3_taps(r_ref, w_ref[0, :, cols], tm)


def _rope(r, cos, sin):
    lane = _iota((1, LANES), 1)
    first = (lane % 32) < 16
    swapped = jnp.where(first, pltpu.roll(r, LANES - 16, 1), pltpu.roll(r, 16, 1))
    return r * cos + swapped * sin


def _inproj_kernel(xm_ref, xp_ref, xn_ref, mod_ref, g_ref, w_ref, wt_ref, gb_ref, qkc_ref, scw_ref, pw_ref,
                   ps_ref, cos_ref, sin_ref,
                   qk_out, vt_out, ogt_out, gates_out, yb_out, aq_out, ak_out, avt_out, yd_out,
                   u_scr, um_scr, r_scr, *, tm, seq):
    i = pl.program_id(1)
    u = _modulated_tile(xm_ref, xp_ref, xn_ref, g_ref[0, 0:1, :], mod_ref[0, 0, 0:1, :], mod_ref[0, 0, 1:2, :],
                        tm, seq)
    u_scr[...] = u.astype(BF16)
    um_scr[...] = u[HALO:HALO + tm].astype(BF16)

    def wcols(name, width, extra=0):
        o = _OFF[name] + extra
        return w_ref[0, :, o:o + width]

    for c, name in enumerate(("q", "k")):
        cols = slice(c * BRANCH_W, (c + 1) * BRANCH_W)
        r_scr[...] = _mm(u_scr[...], wcols(name, BRANCH_W))
        half = _conv3_taps(r_scr, 0.5 * qkc_ref[0, :, cols], tm)
        a = half + half * jnp.tanh(half)
        if name == "k":
            a = a * MLSTM_HD ** -0.5
        qk_out[0, :, cols] = a.astype(BF16)

    um = um_scr[...]

    def trows(name, height):
        o = _TOFF[name]
        return wt_ref[0, o:o + height, :]

    vt_out[0] = _nt(trows("v", BRANCH_W), um).astype(BF16)
    ogt_out[0] = _sigmoid(_nt(trows("o", BRANCH_W), um)).astype(BF16)
    avt_out[0] = _nt(trows("av", LANES), um).astype(BF16)

    g = _nt(trows("gates", 16), um) + gb_ref[0]
    logsig = jnp.minimum(g, 0.0) - jnp.log(1.0 + jnp.exp(-jnp.abs(g)))
    is_forget = (_iota((16, 1), 0) % 8) >= MLSTM_HEADS
    g = jnp.where(is_forget, logsig, g)
    gates_out[0, 0] = g[0:8]
    gates_out[0, 1] = g[8:16]

    r_scr[...] = _mm(u_scr[...], wcols("pool", BRANCH_W))
    t = i * tm + _iota((tm, 1), 0)
    for gi, half in enumerate(POOL_HALF):
        cols = slice(gi * POOL_GW, (gi + 1) * POOL_GW)
        acc = r_scr[HALO - half:HALO - half + tm, cols]
        for o in range(-half + 1, half):
            acc = acc + r_scr[HALO + o:HALO + o + tm, cols]
        cnt = (jnp.minimum(t + half, seq) - jnp.maximum(t - half, 0)).astype(F32)
        delta = acc / cnt - r_scr[HALO:HALO + tm, cols]
        y = _mm(delta.astype(BF16), pw_ref[0, gi]) * ps_ref[0, 0:1, cols]
        yb_out[0, :, cols] = y.astype(BF16)

    cos = cos_ref[...]
    sin = sin_ref[...]
    rq = _mm(um, wcols("aq", BRANCH_W))
    for j in range(BRANCH_W // LANES):
        cols = slice(j * LANES, (j + 1) * LANES)
        aq_out[0, :, cols] = (_rope(rq[:, cols], cos, sin) * ATT_HEAD_DIM ** -0.5).astype(BF16)
    ak_out[0] = _rope(_mm(um, wcols("ak", LANES)), cos, sin).astype(BF16)

    rb = _mm(um, wcols("sb", BRANCH_W))
    r_scr[...] = _mm(u_scr[...], wcols("sc", BRANCH_W)) * _mm(u_scr[...], wcols("sx", BRANCH_W))
    yd_out[0] = (rb * _conv3(r_scr, scw_ref, slice(0, BRANCH_W), tm)).astype(BF16)


def _inproj(x, mod, norm_g, w1, wt, gb, qkc, scw, pw, ps, cos, sin, *, layer, mod_row, tm):
    bsz, seq, d = x.shape
    nt = seq // tm
    main, prev, nxt = _halo_specs(tm, d, seq)
    mrow = (lambda b: b) if mod_row is None else (lambda b: mod_row)
    lsel3 = lambda b, i: (layer, 0, 0)
    in_specs = [
        main, prev, nxt,
        pl.BlockSpec((1, 1, 6, d), lambda b, i: (layer, mrow(b), 0, 0)),
        _const_spec((1, 4, d), lsel3),
        _const_spec((1, d, W1_COLS), lsel3),
        _const_spec((1, WT_ROWS, d), lsel3),
        _const_spec((1, 16, 1), lsel3),
        _const_spec((1, 3, 2 * BRANCH_W), lsel3),
        _const_spec((1, 3, BRANCH_W), lsel3),
        _const_spec((1, len(POOL_HALF), POOL_GW, POOL_GW), lambda b, i: (layer, 0, 0, 0)),
        _const_spec((1, 1, BRANCH_W), lsel3),
        pl.BlockSpec((tm, LANES), lambda b, i: (i, 0)),
        pl.BlockSpec((tm, LANES), lambda b, i: (i, 0)),
    ]
    tok = lambda w: pl.BlockSpec((1, tm, w), lambda b, i: (b, i, 0))
    feat = lambda w: pl.BlockSpec((1, w, tm), lambda b, i: (b, 0, i))
    out_specs = [tok(2 * BRANCH_W), feat(BRANCH_W), feat(BRANCH_W),
                 pl.BlockSpec((1, 2, 8, tm), lambda b, i: (b, 0, 0, i)),
                 tok(BRANCH_W), tok(BRANCH_W), tok(LANES), feat(LANES), tok(BRANCH_W)]
    act = lambda w: jax.ShapeDtypeStruct((bsz, seq, w), BF16)
    act_t = lambda w: jax.ShapeDtypeStruct((bsz, w, seq), BF16)
    out_shape = [act(2 * BRANCH_W), act_t(BRANCH_W), act_t(BRANCH_W),
                 jax.ShapeDtypeStruct((bsz, 2, 8, seq), F32),
                 act(BRANCH_W), act(BRANCH_W), act(LANES), act_t(LANES), act(BRANCH_W)]
    return pl.pallas_call(
        functools.partial(_inproj_kernel, tm=tm, seq=seq),
        grid=(bsz, nt),
        in_specs=in_specs,
        out_specs=out_specs,
        out_shape=out_shape,
        scratch_shapes=[pltpu.VMEM((tm + 2 * HALO, d), BF16), pltpu.VMEM((tm, d), BF16),
                        pltpu.VMEM((tm + 2 * HALO, BRANCH_W), F32)],
        compiler_params=_params("parallel", "parallel"),
        name="inproj",
    )(x, x, x, mod, norm_g, w1, wt, gb, qkc, scw, pw, ps, cos, sin)


def _split3(x):
    x = jnp.concatenate([x, jnp.zeros_like(x)], axis=0)
    hi = x.astype(BF16)
    r1 = x - hi.astype(F32)
    mid = r1.astype(BF16)
    lo = (r1 - mid.astype(F32)).astype(BF16)
    return jnp.concatenate([hi, mid, lo], axis=0)


def _sum3(y):
    return (y[0:8] + y[16:24]) + y[32:40]


def _mlstm_kernel(qk_ref, vt_ref, g_ref, c0_ref, m0_ref, h_out, c_out, m_out, *, chunk, reverse):
    L = chunk
    hd = MLSTM_HD
    nck = qk_ref.shape[1] // L

    @pl.when(pl.program_id(1) == 0)
    def _():
        c_out[...] = c0_ref[...]
        m_out[...] = m0_ref[...]

    s_idx = _iota((L, L), 0)
    t_idx = _iota((L, L), 1)
    seen = (s_idx >= t_idx) if reverse else (s_idx <= t_idx)
    bias = jnp.where(seen, 0.0, -jnp.inf)
    seen_bf = seen.astype(BF16)
    is_last = _iota((1, L), 1) == (0 if reverse else L - 1)
    ones = jnp.ones((hd, L), BF16)
    neg_inf = -jnp.inf

    heads = range(MLSTM_HEADS)
    wide = lambda parts: jnp.concatenate(parts, axis=1)
    toks = [slice(((nck - 1 - i) if reverse else i) * L, (((nck - 1 - i) if reverse else i) + 1) * L)
            for i in range(nck)]

    qs, ks, vaugs, b_rows, ig_rows, keys, scores = [], [], [], [], [], [], []
    for tok in toks:
        g = g_ref[0, 0, :, tok]
        g3 = _split3(g)
        cum_row = _sum3(_mm(g3, seen_bf))
        key_rows = cum_row[MLSTM_HEADS:] - g[:MLSTM_HEADS]
        key_cols = jnp.transpose(jnp.concatenate([key_rows, jnp.zeros_like(key_rows)], axis=0))
        for h in heads:
            qs.append(qk_ref[0, tok, h * hd:(h + 1) * hd])
            ks.append(qk_ref[0, tok, BRANCH_W + h * hd:BRANCH_W + (h + 1) * hd])
            vaugs.append(jnp.concatenate([vt_ref[0, h * hd:(h + 1) * hd, tok], ones], axis=0))
            b_rows.append(cum_row[MLSTM_HEADS + h:MLSTM_HEADS + h + 1, :])
            ig_rows.append(g[h:h + 1, :])
            keys.append(jnp.broadcast_to(key_cols[:, h:h + 1], (L, L)))
            scores.append(_nt(ks[-1], qs[-1]))
    n_pairs = len(qs)
    b_row = wide(b_rows)
    dmat = (b_row - wide(keys)) + wide([bias] * n_pairs)
    m_intra = jnp.max(dmat, axis=0, keepdims=True)
    p = (wide(scores) * jnp.exp(dmat - m_intra)).astype(BF16)
    b_tots = [jnp.max(jnp.where(is_last, b, neg_inf), axis=1, keepdims=True) for b in b_rows]
    u_loc = wide([bt - b + ig for bt, b, ig in zip(b_tots, b_rows, ig_rows)])
    m_locs = [jnp.max(u_loc[:, e * L:(e + 1) * L], axis=1, keepdims=True) for e in range(n_pairs)]
    wk_loc = jnp.exp(u_loc - wide([jnp.broadcast_to(m, (1, L)) for m in m_locs]))
    r_intra = [_mm(vaugs[e], p[:, e * L:(e + 1) * L]) for e in range(n_pairs)]
    kv_loc = [_mm((vaugs[e].astype(F32) * wk_loc[:, e * L:(e + 1) * L]).astype(BF16), ks[e])
              for e in range(n_pairs)]

    for i, tok in enumerate(toks):
        for h in heads:
            e = i * MLSTM_HEADS + h
            lanes = slice(e * L, (e + 1) * L)
            m_prev = m_out[0, h:h + 1, 0:1]
            c_prev = c_out[0, h]
            a = b_rows[e] + m_prev
            mt = jnp.maximum(a, m_intra[:, lanes])
            r = jnp.exp(m_intra[:, lanes] - mt) * r_intra[e] + jnp.exp(a - mt) * _nt(c_prev.astype(BF16), qs[e])
            den = jnp.maximum(jnp.abs(r[hd:]), jnp.exp(-mt))
            h_out[0, h * hd:(h + 1) * hd, tok] = (r[:hd] / den).astype(BF16)
            m_new = jnp.max(jnp.where(is_last, mt, neg_inf), axis=1, keepdims=True)
            c_out[0, h] = (jnp.exp(b_tots[e] + m_prev - m_new) * c_prev
                           + jnp.exp(m_locs[e] - m_new) * kv_loc[e])
            m_out[0, h:h + 1, :] = jnp.broadcast_to(m_new, (1, LANES))


def _mlstm(qk, vt, gates, c0, m0, *, chunk, reverse):
    bsz, seq, _ = qk.shape
    nck = min(MLSTM_CHUNKS_PER_STEP, seq // chunk)
    tb = nck * chunk
    ns = seq // tb
    hd = MLSTM_HD
    direction = 1 if reverse else 0
    blk = (lambda j: ns - 1 - j) if reverse else (lambda j: j)
    state_c = pl.BlockSpec((1, MLSTM_HEADS, 2 * hd, hd), lambda b, j: (b, 0, 0, 0))
    state_m = pl.BlockSpec((1, 8, LANES), lambda b, j: (b, 0, 0))
    return pl.pallas_call(
        functools.partial(_mlstm_kernel, chunk=chunk, reverse=reverse),
        grid=(bsz, ns),
        in_specs=[pl.BlockSpec((1, tb, 2 * BRANCH_W), lambda b, j: (b, blk(j), 0)),
                  pl.BlockSpec((1, BRANCH_W, tb), lambda b, j: (b, 0, blk(j))),
                  pl.BlockSpec((1, 1, 8, tb), lambda b, j: (b, direction, 0, blk(j))),
                  state_c, state_m],
        out_specs=[pl.BlockSpec((1, BRANCH_W, tb), lambda b, j: (b, 0, blk(j))),
                   state_c, state_m],
        out_shape=[jax.ShapeDtypeStruct((bsz, BRANCH_W, seq), BF16),
                   jax.ShapeDtypeStruct((bsz, MLSTM_HEADS, 2 * hd, hd), F32),
                   jax.ShapeDtypeStruct((bsz, 8, LANES), F32)],
        compiler_params=_params("parallel", "arbitrary"),
        name="mlstm_bwd" if reverse else "mlstm_fwd",
    )(qk, vt, gates, c0, m0)


def _attn_kernel(*refs, nb, local):
    blk = ATT_BLOCK
    hd = ATT_HEAD_DIM
    if local:
        q_ref, kp_ref, kc_ref, kn_ref, kx_ref, vp_ref, vc_ref, vn_ref, vx_ref, sink_ref, o_ref = refs
        k_loc = jnp.concatenate([kp_ref[0], kc_ref[0], kn_ref[0]], axis=0)
        vt_loc = jnp.concatenate([vp_ref[0], vc_ref[0], vn_ref[0]], axis=1)
    else:
        q_ref, kx_ref, vx_ref, sink_ref, o_ref = refs
    nsub = q_ref.shape[1] // blk
    step = pl.program_id(1)
    lo = _iota((1, LANES), 1) < hd
    nq = ATT_Q_HEADS * blk
    sk = sink_ref[0]
    if local:
        kj = _iota((blk, nq), 0)
        qi = _iota((blk, nq), 1) % blk
        after = kj >= qi
        before = kj <= qi

    scores, values, prev_bias, next_bias = [], [], [], []
    for i in range(nsub):
        if local:
            k = jnp.concatenate([k_loc[i * blk:(i + 3) * blk], kx_ref[0]], axis=0)
            vt = jnp.concatenate([vt_loc[:, i * blk:(i + 3) * blk], vx_ref[0]], axis=1)
            n = step * nsub + i
            prev_bias.append(jnp.where(after & (n > 0), 0.0, -jnp.inf))
            next_bias.append(jnp.where(before & (n < nb - 1), 0.0, -jnp.inf))
        else:
            k = kx_ref[0]
            vt = vx_ref[0]
        nk = k.shape[0]
        values.append(jnp.concatenate([vt, jnp.ones((16, nk), BF16)], axis=0))
        parts = []
        for j in range(BRANCH_W // LANES):
            q2 = q_ref[0, i * blk:(i + 1) * blk, j * LANES:(j + 1) * LANES]
            zero = jnp.zeros_like(q2)
            parts += [jnp.where(lo, q2, zero), jnp.where(lo, zero, q2)]
        scores.append(_nt(k, jnp.concatenate(parts, axis=0)))
    s = jnp.concatenate(scores, axis=1)
    if local:
        s = jnp.concatenate([s[0:blk] + jnp.concatenate(prev_bias, axis=1), s[blk:2 * blk],
                             s[2 * blk:3 * blk] + jnp.concatenate(next_bias, axis=1), s[3 * blk:]], axis=0)
    sk = jnp.concatenate([sk] * nsub, axis=1)
    m = jnp.maximum(jnp.max(s, axis=0, keepdims=True), sk)
    e = jnp.exp(s - m).astype(BF16)
    extra = jnp.exp(sk - m)
    for i in range(nsub):
        r = _mm(values[i], e[:, i * nq:(i + 1) * nq])
        den = r[2 * hd:2 * hd + 1, :] + extra[:, i * nq:(i + 1) * nq]
        for p in range(ATT_Q_HEADS):
            c = p % 2
            head = p // 2 + 4 * c
            cols = slice(p * blk, (p + 1) * blk)
            o_ref[0, head * hd:(head + 1) * hd, i * blk:(i + 1) * blk] = (
                r[c * hd:(c + 1) * hd, cols] / den[:, cols]).astype(BF16)


def _attention(aq, ak, avt, ak_ctx, avt_ctx, sink, *, layer, local):
    bsz, seq, _ = aq.shape
    nb = seq // ATT_BLOCK
    nsub = min(ATT_SUB, nb)
    tq = nsub * ATT_BLOCK
    lc = ak_ctx.shape[1]
    qspec = pl.BlockSpec((1, tq, BRANCH_W), lambda b, n: (b, n, 0))
    kx_spec = pl.BlockSpec((1, lc, LANES), lambda b, n: (b, 0, 0))
    vx_spec = pl.BlockSpec((1, LANES, lc), lambda b, n: (b, 0, 0))
    sink_spec = pl.BlockSpec((1, 1, ATT_Q_HEADS * ATT_BLOCK), lambda b, n: (layer, 0, 0))
    if local:
        prev = lambda n: jnp.maximum(n * nsub - 1, 0)
        nxt = lambda n: jnp.minimum((n + 1) * nsub, nb - 1)
        kspecs = [pl.BlockSpec((1, ATT_BLOCK, LANES), lambda b, n: (b, prev(n), 0)),
                  pl.BlockSpec((1, tq, LANES), lambda b, n: (b, n, 0)),
                  pl.BlockSpec((1, ATT_BLOCK, LANES), lambda b, n: (b, nxt(n), 0))]
        vspecs = [pl.BlockSpec((1, LANES, ATT_BLOCK), lambda b, n: (b, 0, prev(n))),
                  pl.BlockSpec((1, LANES, tq), lambda b, n: (b, 0, n)),
                  pl.BlockSpec((1, LANES, ATT_BLOCK), lambda b, n: (b, 0, nxt(n)))]
        in_specs = [qspec, *kspecs, kx_spec, *vspecs, vx_spec, sink_spec]
        args = (aq, ak, ak, ak, ak_ctx, avt, avt, avt, avt_ctx, sink)
    else:
        in_specs = [qspec, kx_spec, vx_spec, sink_spec]
        args = (aq, ak_ctx, avt_ctx, sink)
    return pl.pallas_call(
        functools.partial(_attn_kernel, nb=nb, local=local),
        grid=(bsz, nb // nsub),
        in_specs=in_specs,
        out_specs=pl.BlockSpec((1, BRANCH_W, tq), lambda b, n: (b, 0, n)),
        out_shape=jax.ShapeDtypeStruct((bsz, BRANCH_W, seq), BF16),
        compiler_params=_params("parallel", "parallel"),
        name="attention",
    )(*args)


def _merge_kernel(x_ref, mod_ref, g_ref, hf_ref, hb_ref, og_ref, yb_ref, att_ref, yd_ref, wg_ref, wb_ref, wo_ref,
                  o_ref):
    d = D_MODEL
    x = x_ref[0]
    u = (_rms(x) * g_ref[0, 0:1, :] * (1.0 + mod_ref[0, 0, 1:2, :]) + mod_ref[0, 0, 0:1, :]).astype(BF16)
    ya_t = (og_ref[0].astype(F32) * (hf_ref[0].astype(F32) + hb_ref[0].astype(F32))).astype(BF16)
    ys = (ya_t, yb_ref[0], att_ref[0], yd_ref[0])
    feature_major = (True, False, True, False)
    merged = None
    for n in range(N_BRANCH):
        gate = _sigmoid(_mm(u, wg_ref[0, :, n * d:(n + 1) * d]))
        proj = _tn(ys[n], wb_ref[0, n]) if feature_major[n] else _mm(ys[n], wb_ref[0, n])
        term = gate * proj
        merged = term if merged is None else merged + term
    out = _mm(merged.astype(BF16), wo_ref[0])
    o_ref[0] = x + mod_ref[0, 0, 2:3, :] * (_rms(out) * g_ref[0, 1:2, :])


def _merge(x, mod, norm_g, hf, hb, og, yb, att, yd, w_mg, w_branch, w_out, *, layer, mod_row, tm):
    bsz, seq, d = x.shape
    mrow = (lambda b: b) if mod_row is None else (lambda b: mod_row)
    tok = lambda w: pl.BlockSpec((1, tm, w), lambda b, i: (b, i, 0))
    feat = pl.BlockSpec((1, BRANCH_W, tm), lambda b, i: (b, 0, i))
    lsel3 = lambda b, i: (layer, 0, 0)
    return pl.pallas_call(
        _merge_kernel,
        grid=(bsz, seq // tm),
        in_specs=[tok(d),
                  pl.BlockSpec((1, 1, 6, d), lambda b, i: (layer, mrow(b), 0, 0)),
                  _const_spec((1, 4, d), lsel3),
                  feat, feat, feat, tok(BRANCH_W), feat, tok(BRANCH_W),
                  _const_spec((1, d, N_BRANCH * d), lsel3),
                  _const_spec((1, N_BRANCH, BRANCH_W, d), lambda b, i: (layer, 0, 0, 0)),
                  _const_spec((1, d, d), lsel3)],
        out_specs=tok(d),
        out_shape=jax.ShapeDtypeStruct((bsz, seq, d), F32),
        compiler_params=_params("parallel", "parallel"),
        name="merge",
    )(x, mod, norm_g, hf, hb, og, yb, att, yd, w_mg, w_branch, w_out)


def _ffn_kernel(xm_ref, xp_ref, xn_ref, mod_ref, g_ref, wu_ref, wc_ref, wd_ref, o_ref, u_scr, rg_scr, rv_scr,
                h_scr, *, tm, seq):
    u = _modulated_tile(xm_ref, xp_ref, xn_ref, g_ref[0, 2:3, :], mod_ref[0, 0, 3:4, :], mod_ref[0, 0, 4:5, :],
                        tm, seq)
    u_scr[...] = u.astype(BF16)
    cw = FF_CHUNK
    for c in range(D_FF // cw):
        gcols = slice(c * cw, (c + 1) * cw)
        vcols = slice(D_FF + c * cw, D_FF + (c + 1) * cw)
        rg_scr[...] = _mm(u_scr[...], wu_ref[0, :, gcols])
        rv_scr[...] = _mm(u_scr[...], wu_ref[0, :, vcols])
        half = _conv3_taps(rg_scr, 0.5 * wc_ref[0, :, gcols], tm)
        val = _conv3(rv_scr, wc_ref, vcols, tm)
        h_scr[:, gcols] = ((half + half * jnp.tanh(half)) * val).astype(BF16)
    out = _mm(h_scr[...], wd_ref[0])
    x = xm_ref[0]
    o_ref[0] = x + mod_ref[0, 0, 5:6, :] * (_rms(out) * g_ref[0, 3:4, :])


def _ffn(x, mod, norm_g, w_up, w_conv, w_down, *, layer, mod_row, tm):
    bsz, seq, d = x.shape
    main, prev, nxt = _halo_specs(tm, d, seq)
    mrow = (lambda b: b) if mod_row is None else (lambda b: mod_row)
    lsel3 = lambda b, i: (layer, 0, 0)
    return pl.pallas_call(
        functools.partial(_ffn_kernel, tm=tm, seq=seq),
        grid=(bsz, seq // tm),
        in_specs=[main, prev, nxt,
                  pl.BlockSpec((1, 1, 6, d), lambda b, i: (layer, mrow(b), 0, 0)),
                  _const_spec((1, 4, d), lsel3),
                  _const_spec((1, d, 2 * D_FF), lsel3),
                  _const_spec((1, 3, 2 * D_FF), lsel3),
                  _const_spec((1, D_FF, d), lsel3)],
        out_specs=main,
        out_shape=jax.ShapeDtypeStruct((bsz, seq, d), F32),
        scratch_shapes=[pltpu.VMEM((tm + 2 * HALO, d), BF16), pltpu.VMEM((tm + 2 * HALO, FF_CHUNK), F32),
                        pltpu.VMEM((tm + 2 * HALO, FF_CHUNK), F32), pltpu.VMEM((tm, D_FF), BF16)],
        compiler_params=_params("parallel", "parallel"),
        name="convffn",
    )(x, x, x, mod, norm_g, w_up, w_conv, w_down)


def _rope_tables(seq):
    t = jnp.arange(seq, dtype=jnp.int32)[:, None]
    lane = jnp.arange(LANES, dtype=jnp.int32)[None, :]
    in_head = lane % ATT_HEAD_DIM
    n_freq = ATT_HEAD_DIM // 4
    inv_freq = ROPE_BASE ** (-(in_head % n_freq).astype(F32) / n_freq)
    pos = jnp.where(in_head < ATT_HEAD_DIM // 2, t // GRID_W, t % GRID_W).astype(F32)
    ang = pos * inv_freq
    sign = jnp.where((in_head % (2 * n_freq)) < n_freq, -1.0, 1.0)
    return jnp.cos(ang), sign * jnp.sin(ang)


def _tile_rows(seq):
    return min(seq, 512)


def kernel(x, c, ctx, c_ctx, w_mod, b_mod, norm_g, w_in, mlstm_qk_conv, mlstm_gate_bias, pool_w, pool_scale,
           attn_sink, sconv_w, w_branch, w_out, ffn_up, ffn_conv, ffn_down):
    bsz, seq, d = x.shape
    lc = ctx.shape[1]
    depth = w_mod.shape[0]
    assert d == D_MODEL and seq % 512 == 0 and lc % 256 == 0 and bsz + 1 <= 8

    cond = jnp.concatenate([c, c_ctx[None, :], jnp.zeros((8 - bsz - 1, d), F32)], axis=0)
    mod = _modulation(cond, w_mod, b_mod).reshape(depth, 8, 6, d)

    g0 = BRANCH_W * 4
    g1 = g0 + 4 * MLSTM_HEADS
    mg0 = w_in.shape[-1] - N_BRANCH * d
    p0 = g1
    aq0 = p0 + BRANCH_W
    ak0 = aq0 + BRANCH_W
    av0 = ak0 + ATT_KV_HEADS * ATT_HEAD_DIM
    sc0 = av0 + ATT_KV_HEADS * ATT_HEAD_DIM
    w_aq = w_in[:, :, aq0:ak0].reshape(depth, d, ATT_KV_HEADS, ATT_Q_HEADS // ATT_KV_HEADS, ATT_HEAD_DIM)
    w_aq = jnp.swapaxes(w_aq, 2, 3).reshape(depth, d, BRANCH_W)
    v0 = 2 * BRANCH_W
    w1 = jnp.concatenate([w_in[:, :, :v0], w_in[:, :, p0:aq0], w_aq, w_in[:, :, ak0:av0], w_in[:, :, sc0:mg0]],
                         axis=-1).astype(BF16)
    perm = jnp.array(_GATE_PERM)
    wt = jnp.concatenate([w_in[:, :, v0:g0], w_in[:, :, av0:sc0], w_in[:, :, g0:g1][:, :, perm]], axis=-1)
    wt = jnp.swapaxes(wt, 1, 2).astype(BF16)
    gb = mlstm_gate_bias.reshape(depth, 4 * MLSTM_HEADS)[:, perm][:, :, None]
    w_mg = w_in[:, :, mg0:].astype(BF16)
    pw = pool_w.astype(BF16)
    ps = pool_scale[:, None, :]
    sink_heads = attn_sink.reshape(depth, ATT_KV_HEADS, ATT_Q_HEADS // ATT_KV_HEADS).swapaxes(1, 2)
    sink = jnp.broadcast_to(sink_heads.reshape(depth, 1, ATT_Q_HEADS, 1),
                            (depth, 1, ATT_Q_HEADS, ATT_BLOCK)).reshape(depth, 1, ATT_Q_HEADS * ATT_BLOCK)
    wb = w_branch.astype(BF16)
    wo = w_out.astype(BF16)
    wu = ffn_up.astype(BF16)
    wc = ffn_conv
    wd = ffn_down.astype(BF16)

    cos_x, sin_x = _rope_tables(seq)
    cos_c, sin_c = jnp.ones((lc, LANES), F32), jnp.zeros((lc, LANES), F32)

    hd = MLSTM_HD
    zero_c = jnp.zeros((bsz, MLSTM_HEADS, 2 * hd, hd), F32)
    zero_m = jnp.zeros((bsz, 8, LANES), F32)
    tm_x, tm_c = _tile_rows(seq), _tile_rows(lc)
    scan = functools.partial(_mlstm, chunk=MLSTM_CHUNK)

    h = ctx
    for l in range(depth):
        ctx_out = l < depth - 1
        proj = functools.partial(_inproj, mod=mod, norm_g=norm_g, w1=w1, wt=wt, gb=gb, qkc=mlstm_qk_conv,
                                 scw=sconv_w, pw=pw, ps=ps, layer=l)
        qk_c, v_c, og_c, gt_c, yb_c, aq_c, ak_c, avt_c, yd_c = proj(h, cos=cos_c, sin=sin_c, mod_row=bsz, tm=tm_c)
        qk_x, v_x, og_x, gt_x, yb_x, aq_x, ak_x, avt_x, yd_x = proj(x, cos=cos_x, sin=sin_x, mod_row=None, tm=tm_x)
        hf_c, cf, mf = scan(qk_c, v_c, gt_c, zero_c, zero_m, reverse=False)
        hb_c, cb, mb = scan(qk_c, v_c, gt_c, zero_c, zero_m, reverse=True)
        hf_x, _, _ = scan(qk_x, v_x, gt_x, cf, mf, reverse=False)
        hb_x, _, _ = scan(qk_x, v_x, gt_x, cb, mb, reverse=True)
        att_x = _attention(aq_x, ak_x, avt_x, ak_c, avt_c, sink, layer=l, local=True)
        tail = dict(norm_g=norm_g, layer=l)
        x = _merge(x, mod, hf=hf_x, hb=hb_x, og=og_x, yb=yb_x, att=att_x, yd=yd_x, w_mg=w_mg, w_branch=wb,
                   w_out=wo, mod_row=None, tm=tm_x, **tail)
        x = _ffn(x, mod, w_up=wu, w_conv=wc, w_down=wd, mod_row=None, tm=tm_x, **tail)
        if ctx_out:
            att_c = _attention(aq_c, ak_c, avt_c, ak_c, avt_c, sink, layer=l, local=False)
            h = _merge(h, mod, hf=hf_c, hb=hb_c, og=og_c, yb=yb_c, att=att_c, yd=yd_c, w_mg=w_mg, w_branch=wb,
                       w_out=wo, mod_row=bsz, tm=tm_c, **tail)
            h = _ffn(h, mod, w_up=wu, w_conv=wc, w_down=wd, mod_row=bsz, tm=tm_c, **tail)
    return x
```

```python
import functools

import jax
import jax.numpy as jnp
from jax import lax
from jax.experimental import pallas as pl
from jax.experimental.pallas import tpu as pltpu

F32 = jnp.float32
BF16 = jnp.bfloat16
HIGHEST = lax.Precision.HIGHEST

D_MODEL = 1024
GRID_W = 64
N_BRANCH = 4
BRANCH_W = D_MODEL // 2
MLSTM_HEADS = 4
MLSTM_HD = BRANCH_W // MLSTM_HEADS
MLSTM_CHUNK = 256
MLSTM_CHUNKS_PER_STEP = 8
POOL_HALF = (1, 2, 4, 8)
POOL_GW = BRANCH_W // len(POOL_HALF)
ATT_HEAD_DIM = 64
ATT_Q_HEADS = BRANCH_W // ATT_HEAD_DIM
ATT_KV_HEADS = 2
ATT_BLOCK = 128
ATT_SUB = 4
LOG2E = 1.4426950408889634
ATT_Q_SCALE = ATT_HEAD_DIM ** -0.5 * LOG2E
ROPE_BASE = 10000.0
D_FF = 2816
EPS = 1e-6

LANES = 128
HALO = 8
FF_CHUNK = 256
VMEM_LIMIT = 56 * 1024 * 1024

_OFF = {}
_acc = 0
for _name, _w in (("q", 512), ("k", 512), ("pool", 512), ("aq", 512), ("ak", 128), ("sb", 512), ("sc", 512),
                  ("sx", 512)):
    _OFF[_name] = _acc
    _acc += _w
W1_COLS = _acc
_TOFF = {}
_acc = 0
for _name, _w in (("v", 512), ("o", 512), ("av", 128), ("gates", 16)):
    _TOFF[_name] = _acc
    _acc += _w
WT_ROWS = _acc
_GATE_PERM = (0, 1, 2, 3, 8, 9, 10, 11, 4, 5, 6, 7, 12, 13, 14, 15)


def _iota(shape, dim):
    return lax.broadcasted_iota(jnp.int32, shape, dim)


def _rms(x):
    return x * lax.rsqrt(jnp.mean(x * x, axis=-1, keepdims=True) + EPS)


def _sigmoid(x):
    return 1.0 / (1.0 + jnp.exp(-x))


def _nt(a, b, **kw):
    return lax.dot_general(a, b, (((1,), (1,)), ((), ())), preferred_element_type=F32, **kw)


def _tn(a, b, **kw):
    return lax.dot_general(a, b, (((0,), (0,)), ((), ())), preferred_element_type=F32, **kw)


def _mm(a, b, **kw):
    return jnp.dot(a, b, preferred_element_type=F32, **kw)


def _params(*sem, flags=None):
    return pltpu.CompilerParams(dimension_semantics=sem, vmem_limit_bytes=VMEM_LIMIT, flags=flags)


def _const_spec(shape, index_map):
    return pl.BlockSpec(shape, index_map, pipeline_mode=pl.Buffered(1))


def _mod_kernel(c_ref, w_ref, b_ref, o_ref):
    cond = c_ref[...]
    sc = cond * _sigmoid(cond)
    o_ref[0] = _mm(sc, w_ref[0], precision=HIGHEST) + b_ref[0]


def _modulation(cond, w_mod, b_mod):
    depth, d, n = w_mod.shape
    rows = cond.shape[0]
    tn = 1024
    return pl.pallas_call(
        _mod_kernel,
        grid=(depth, n // tn),
        in_specs=[pl.BlockSpec((rows, d), lambda l, j: (0, 0)),
                  pl.BlockSpec((1, d, tn), lambda l, j: (l, 0, j)),
                  pl.BlockSpec((1, 1, tn), lambda l, j: (l, 0, j))],
        out_specs=pl.BlockSpec((1, rows, tn), lambda l, j: (l, 0, j)),
        out_shape=jax.ShapeDtypeStruct((depth, rows, n), F32),
        compiler_params=_params("parallel", "parallel"),
        name="modulation",
    )(cond, w_mod, b_mod.reshape(depth, 1, n))


def _halo_specs(tm, d, seq):
    nh = seq // HALO
    per = tm // HALO
    main = pl.BlockSpec((1, tm, d), lambda b, i: (b, i, 0))
    prev = pl.BlockSpec((1, HALO, d), lambda b, i: (b, jnp.maximum(i * per - 1, 0), 0))
    nxt = pl.BlockSpec((1, HALO, d), lambda b, i: (b, jnp.minimum((i + 1) * per, nh - 1), 0))
    return main, prev, nxt


def _modulated_tile(xm_ref, xp_ref, xn_ref, gain, shift, scale, tm, seq):
    i = pl.program_id(1)
    mult = gain * (1.0 + scale)
    mod = lambda x: _rms(x) * mult + shift
    before = jnp.where(i > 0, mod(xp_ref[0]), 0.0)
    after = jnp.where((i + 1) * tm < seq, mod(xn_ref[0]), 0.0)
    return jnp.concatenate([before, mod(xm_ref[0]), after], axis=0)


def _conv3_taps(r_ref, w, tm):
    return (r_ref[HALO - 1:HALO - 1 + tm, :] * w[0:1] + r_ref[HALO:HALO + tm, :] * w[1:2]
            + r_ref[HALO + 1:HALO + 1 + tm, :] * w[2:3])


def _conv3(r_ref, w_ref, cols, tm):
    return _conv3_taps(r_ref, w_ref[0, :, cols], tm)


def _rope(r, cos, sin):
    lane = _iota((1, LANES), 1)
    first = (lane % 32) < 16
    swapped = jnp.where(first, pltpu.roll(r, LANES - 16, 1), pltpu.roll(r, 16, 1))
    return r * cos + swapped * sin


def _inproj_kernel(xm_ref, xp_ref, xn_ref, mod_ref, g_ref, w_ref, wt_ref, gb_ref, qkc_ref, scw_ref, pw_ref,
                   ps_ref, cos_ref, sin_ref,
                   qk_out, vt_out, ogt_out, gates_out, yb_out, aq_out, ak_out, avt_out, yd_out,
                   u_scr, um_scr, r_scr, *, tm, seq):
    i = pl.program_id(1)
    u = _modulated_tile(xm_ref, xp_ref, xn_ref, g_ref[0, 0:1, :], mod_ref[0, 0, 0:1, :], mod_ref[0, 0, 1:2, :],
                        tm, seq)
    u_scr[...] = u.astype(BF16)
    um_scr[...] = u[HALO:HALO + tm].astype(BF16)

    def wcols(name, width, extra=0):
        o = _OFF[name] + extra
        return w_ref[0, :, o:o + width]

    for c, name in enumerate(("q", "k")):
        cols = slice(c * BRANCH_W, (c + 1) * BRANCH_W)
        r_scr[...] = _mm(u_scr[...], wcols(name, BRANCH_W))
        half = _conv3_taps(r_scr, 0.5 * qkc_ref[0, :, cols], tm)
        a = half + half * jnp.tanh(half)
        if name == "k":
            a = a * MLSTM_HD ** -0.5
        qk_out[0, :, cols] = a.astype(BF16)

    um = um_scr[...]

    def trows(name, height):
        o = _TOFF[name]
        return wt_ref[0, o:o + height, :]

    vt_out[0] = _nt(trows("v", BRANCH_W), um).astype(BF16)
    ogt_out[0] = _sigmoid(_nt(trows("o", BRANCH_W), um)).astype(BF16)
    avt_out[0] = _nt(trows("av", LANES), um).astype(BF16)

    g = _nt(trows("gates", 16), um) + gb_ref[0]
    logsig = jnp.minimum(g, 0.0) - jnp.log(1.0 + jnp.exp(-jnp.abs(g)))
    is_forget = (_iota((16, 1), 0) % 8) >= MLSTM_HEADS
    g = jnp.where(is_forget, logsig, g) * LOG2E
    gates_out[0, 0] = g[0:8]
    gates_out[0, 1] = g[8:16]

    r_scr[...] = _mm(u_scr[...], wcols("pool", BRANCH_W))
    assert max(POOL_HALF) <= HALO
    t_top = i * tm + _iota((HALO, 1), 0)
    t_bot = t_top + (tm - HALO)
    for gi, half in enumerate(POOL_HALF):
        cols = slice(gi * POOL_GW, (gi + 1) * POOL_GW)
        acc = r_scr[HALO - half:HALO - half + tm, cols]
        for o in range(-half + 1, half):
            acc = acc + r_scr[HALO + o:HALO + o + tm, cols]
        inv = lambda t: 1.0 / (jnp.minimum(t + half, seq) - jnp.maximum(t - half, 0)).astype(F32)
        mean = jnp.concatenate([acc[:HALO] * inv(t_top), acc[HALO:tm - HALO] * (0.5 / half),
                                acc[tm - HALO:] * inv(t_bot)], axis=0)
        delta = mean - r_scr[HALO:HALO + tm, cols]
        y = _mm(delta.astype(BF16), pw_ref[0, gi]) * ps_ref[0, 0:1, cols]
        yb_out[0, :, cols] = y.astype(BF16)

    cos = cos_ref[...]
    sin = sin_ref[...]
    rq = _mm(um, wcols("aq", BRANCH_W))
    for j in range(BRANCH_W // LANES):
        cols = slice(j * LANES, (j + 1) * LANES)
        aq_out[0, :, cols] = (_rope(rq[:, cols], cos, sin) * ATT_Q_SCALE).astype(BF16)
    ak_out[0] = _rope(_mm(um, wcols("ak", LANES)), cos, sin).astype(BF16)

    rb = _mm(um, wcols("sb", BRANCH_W))
    r_scr[...] = _mm(u_scr[...], wcols("sc", BRANCH_W)) * _mm(u_scr[...], wcols("sx", BRANCH_W))
    yd_out[0] = (rb * _conv3(r_scr, scw_ref, slice(0, BRANCH_W), tm)).astype(BF16)


def _inproj(x, mod, norm_g, w1, wt, gb, qkc, scw, pw, ps, cos, sin, *, layer, mod_row, tm):
    bsz, seq, d = x.shape
    nt = seq // tm
    main, prev, nxt = _halo_specs(tm, d, seq)
    mrow = (lambda b: b) if mod_row is None else (lambda b: mod_row)
    lsel3 = lambda b, i: (layer, 0, 0)
    in_specs = [
        main, prev, nxt,
        pl.BlockSpec((1, 1, 6, d), lambda b, i: (layer, mrow(b), 0, 0)),
        _const_spec((1, 4, d), lsel3),
        _const_spec((1, d, W1_COLS), lsel3),
        _const_spec((1, WT_ROWS, d), lsel3),
        _const_spec((1, 16, 1), lsel3),
        _const_spec((1, 3, 2 * BRANCH_W), lsel3),
        _const_spec((1, 3, BRANCH_W), lsel3),
        _const_spec((1, len(POOL_HALF), POOL_GW, POOL_GW), lambda b, i: (layer, 0, 0, 0)),
        _const_spec((1, 1, BRANCH_W), lsel3),
        pl.BlockSpec((tm, LANES), lambda b, i: (i, 0)),
        pl.BlockSpec((tm, LANES), lambda b, i: (i, 0)),
    ]
    tok = lambda w: pl.BlockSpec((1, tm, w), lambda b, i: (b, i, 0))
    feat = lambda w: pl.BlockSpec((1, w, tm), lambda b, i: (b, 0, i))
    out_specs = [tok(2 * BRANCH_W), feat(BRANCH_W), feat(BRANCH_W),
                 pl.BlockSpec((1, 2, 8, tm), lambda b, i: (b, 0, 0, i)),
                 tok(BRANCH_W), tok(BRANCH_W), tok(LANES), feat(LANES), tok(BRANCH_W)]
    act = lambda w: jax.ShapeDtypeStruct((bsz, seq, w), BF16)
    act_t = lambda w: jax.ShapeDtypeStruct((bsz, w, seq), BF16)
    out_shape = [act(2 * BRANCH_W), act_t(BRANCH_W), act_t(BRANCH_W),
                 jax.ShapeDtypeStruct((bsz, 2, 8, seq), F32),
                 act(BRANCH_W), act(BRANCH_W), act(LANES), act_t(LANES), act(BRANCH_W)]
    return pl.pallas_call(
        functools.partial(_inproj_kernel, tm=tm, seq=seq),
        grid=(bsz, nt),
        in_specs=in_specs,
        out_specs=out_specs,
        out_shape=out_shape,
        scratch_shapes=[pltpu.VMEM((tm + 2 * HALO, d), BF16), pltpu.VMEM((tm, d), BF16),
                        pltpu.VMEM((tm + 2 * HALO, BRANCH_W), F32)],
        compiler_params=_params("parallel", "parallel"),
        name="inproj",
    )(x, x, x, mod, norm_g, w1, wt, gb, qkc, scw, pw, ps, cos, sin)


def _split3(x):
    x = jnp.concatenate([x, jnp.zeros_like(x)], axis=0)
    hi = x.astype(BF16)
    r1 = x - hi.astype(F32)
    mid = r1.astype(BF16)
    lo = (r1 - mid.astype(F32)).astype(BF16)
    return jnp.concatenate([hi, mid, lo], axis=0)


def _sum3(y):
    return (y[0:8] + y[16:24]) + y[32:40]


def _mlstm_kernel(qk_ref, vt_ref, g_ref, c0_ref, m0_ref, h_out, c_out, m_out, *, chunk, reverse):
    L = chunk
    hd = MLSTM_HD
    nck = qk_ref.shape[1] // L

    @pl.when(pl.program_id(1) == 0)
    def _():
        c_out[...] = c0_ref[...]
        m_out[...] = m0_ref[...]

    s_idx = _iota((L, L), 0)
    t_idx = _iota((L, L), 1)
    seen = (s_idx >= t_idx) if reverse else (s_idx <= t_idx)
    bias = jnp.where(seen, 0.0, -jnp.inf)
    seen_bf = seen.astype(BF16)
    is_last = _iota((1, L), 1) == (0 if reverse else L - 1)
    ones = jnp.ones((hd, L), BF16)
    neg_inf = -jnp.inf

    heads = range(MLSTM_HEADS)
    wide = lambda parts: jnp.concatenate(parts, axis=1)
    toks = [slice(((nck - 1 - i) if reverse else i) * L, (((nck - 1 - i) if reverse else i) + 1) * L)
            for i in range(nck)]

    qs, ks, vaugs, b_rows, ig_rows, keys, scores = [], [], [], [], [], [], []
    for tok in toks:
        g = g_ref[0, 0, :, tok]
        g3 = _split3(g)
        cum_row = _sum3(_mm(g3, seen_bf))
        key_rows = cum_row[MLSTM_HEADS:] - g[:MLSTM_HEADS]
        key_cols = jnp.transpose(jnp.concatenate([key_rows, jnp.zeros_like(key_rows)], axis=0))
        for h in heads:
            qs.append(qk_ref[0, tok, h * hd:(h + 1) * hd])
            ks.append(qk_ref[0, tok, BRANCH_W + h * hd:BRANCH_W + (h + 1) * hd])
            vaugs.append(jnp.concatenate([vt_ref[0, h * hd:(h + 1) * hd, tok], ones], axis=0))
            b_rows.append(cum_row[MLSTM_HEADS + h:MLSTM_HEADS + h + 1, :])
            ig_rows.append(g[h:h + 1, :])
            keys.append(jnp.broadcast_to(key_cols[:, h:h + 1], (L, L)))
            scores.append(_nt(ks[-1], qs[-1]))
    n_pairs = len(qs)
    b_row = wide(b_rows)
    dmat = (b_row - wide(keys)) + wide([bias] * n_pairs)
    m_intra = jnp.max(dmat, axis=0, keepdims=True)
    p = (wide(scores) * jnp.exp2(dmat - m_intra)).astype(BF16)
    b_tots = [jnp.max(jnp.where(is_last, b, neg_inf), axis=1, keepdims=True) for b in b_rows]
    u_loc = wide([bt - b + ig for bt, b, ig in zip(b_tots, b_rows, ig_rows)])
    m_locs = [jnp.max(u_loc[:, e * L:(e + 1) * L], axis=1, keepdims=True) for e in range(n_pairs)]
    wk_loc = jnp.exp2(u_loc - wide([jnp.broadcast_to(m, (1, L)) for m in m_locs]))
    r_intra = [_mm(vaugs[e], p[:, e * L:(e + 1) * L]) for e in range(n_pairs)]
    kv_loc = [_mm((vaugs[e].astype(F32) * wk_loc[:, e * L:(e + 1) * L]).astype(BF16), ks[e])
              for e in range(n_pairs)]

    for i, tok in enumerate(toks):
        for h in heads:
            e = i * MLSTM_HEADS + h
            lanes = slice(e * L, (e + 1) * L)
            m_prev = m_out[0, h:h + 1, 0:1]
            c_prev = c_out[0, h]
            a = b_rows[e] + m_prev
            mt = jnp.maximum(a, m_intra[:, lanes])
            r = (jnp.exp2(m_intra[:, lanes] - mt) * r_intra[e]
                 + jnp.exp2(a - mt) * _nt(c_prev.astype(BF16), qs[e]))
            den = jnp.maximum(jnp.abs(r[hd:]), jnp.exp2(-mt))
            h_out[0, h * hd:(h + 1) * hd, tok] = (r[:hd] / den).astype(BF16)
            m_new = jnp.max(jnp.where(is_last, mt, neg_inf), axis=1, keepdims=True)
            c_out[0, h] = (jnp.exp2(b_tots[e] + m_prev - m_new) * c_prev
                           + jnp.exp2(m_locs[e] - m_new) * kv_loc[e])
            m_out[0, h:h + 1, :] = jnp.broadcast_to(m_new, (1, LANES))


def _mlstm(qk, vt, gates, c0, m0, *, chunk, reverse):
    bsz, seq, _ = qk.shape
    nck = min(MLSTM_CHUNKS_PER_STEP, seq // chunk)
    tb = nck * chunk
    ns = seq // tb
    hd = MLSTM_HD
    direction = 1 if reverse else 0
    blk = (lambda j: ns - 1 - j) if reverse else (lambda j: j)
    state_c = pl.BlockSpec((1, MLSTM_HEADS, 2 * hd, hd), lambda b, j: (b, 0, 0, 0))
    state_m = pl.BlockSpec((1, 8, LANES), lambda b, j: (b, 0, 0))
    return pl.pallas_call(
        functools.partial(_mlstm_kernel, chunk=chunk, reverse=reverse),
        grid=(bsz, ns),
        in_specs=[pl.BlockSpec((1, tb, 2 * BRANCH_W), lambda b, j: (b, blk(j), 0)),
                  pl.BlockSpec((1, BRANCH_W, tb), lambda b, j: (b, 0, blk(j))),
                  pl.BlockSpec((1, 1, 8, tb), lambda b, j: (b, direction, 0, blk(j))),
                  state_c, state_m],
        out_specs=[pl.BlockSpec((1, BRANCH_W, tb), lambda b, j: (b, 0, blk(j))),
                   state_c, state_m],
        out_shape=[jax.ShapeDtypeStruct((bsz, BRANCH_W, seq), BF16),
                   jax.ShapeDtypeStruct((bsz, MLSTM_HEADS, 2 * hd, hd), F32),
                   jax.ShapeDtypeStruct((bsz, 8, LANES), F32)],
        compiler_params=_params("parallel", "arbitrary"),
        name="mlstm_bwd" if reverse else "mlstm_fwd",
    )(qk, vt, gates, c0, m0)


def _attn_kernel(*refs, nb, local):
    blk = ATT_BLOCK
    hd = ATT_HEAD_DIM
    if local:
        q_ref, kp_ref, kc_ref, kn_ref, kx_ref, vp_ref, vc_ref, vn_ref, vx_ref, sink_ref, o_ref = refs
        k_loc = jnp.concatenate([kp_ref[0], kc_ref[0], kn_ref[0]], axis=0)
        vt_loc = jnp.concatenate([vp_ref[0], vc_ref[0], vn_ref[0]], axis=1)
    else:
        q_ref, kx_ref, vx_ref, sink_ref, o_ref = refs
    nsub = q_ref.shape[1] // blk
    step = pl.program_id(1)
    lo = _iota((1, LANES), 1) < hd
    nq = ATT_Q_HEADS * blk
    sk = sink_ref[0] * LOG2E
    if local:
        kj = _iota((blk, nq), 0)
        qi = _iota((blk, nq), 1) % blk
        after = kj >= qi
        before = kj <= qi

    scores, values, prev_bias, next_bias = [], [], [], []
    for i in range(nsub):
        if local:
            k = jnp.concatenate([k_loc[i * blk:(i + 3) * blk], kx_ref[0]], axis=0)
            vt = jnp.concatenate([vt_loc[:, i * blk:(i + 3) * blk], vx_ref[0]], axis=1)
            n = step * nsub + i
            prev_bias.append(jnp.where(after & (n > 0), 0.0, -jnp.inf))
            next_bias.append(jnp.where(before & (n < nb - 1), 0.0, -jnp.inf))
        else:
            k = kx_ref[0]
            vt = vx_ref[0]
        nk = k.shape[0]
        values.append(jnp.concatenate([vt, jnp.ones((16, nk), BF16)], axis=0))
        parts = []
        for j in range(BRANCH_W // LANES):
            q2 = q_ref[0, i * blk:(i + 1) * blk, j * LANES:(j + 1) * LANES]
            zero = jnp.zeros_like(q2)
            parts += [jnp.where(lo, q2, zero), jnp.where(lo, zero, q2)]
        scores.append(_nt(k, jnp.concatenate(parts, axis=0)))
    s = jnp.concatenate(scores, axis=1)
    if local:
        s = jnp.concatenate([s[0:blk] + jnp.concatenate(prev_bias, axis=1), s[blk:2 * blk],
                             s[2 * blk:3 * blk] + jnp.concatenate(next_bias, axis=1), s[3 * blk:]], axis=0)
    sk = jnp.concatenate([sk] * nsub, axis=1)
    m = jnp.maximum(jnp.max(s, axis=0, keepdims=True), sk)
    e = jnp.exp2(s - m).astype(BF16)
    extra = jnp.exp2(sk - m)
    for i in range(nsub):
        r = _mm(values[i], e[:, i * nq:(i + 1) * nq])
        den = r[2 * hd:2 * hd + 1, :] + extra[:, i * nq:(i + 1) * nq]
        for p in range(ATT_Q_HEADS):
            c = p % 2
            head = p // 2 + 4 * c
            cols = slice(p * blk, (p + 1) * blk)
            o_ref[0, head * hd:(head + 1) * hd, i * blk:(i + 1) * blk] = (
                r[c * hd:(c + 1) * hd, cols] / den[:, cols]).astype(BF16)


def _attention(aq, ak, avt, ak_ctx, avt_ctx, sink, *, layer, local):
    bsz, seq, _ = aq.shape
    nb = seq // ATT_BLOCK
    nsub = min(ATT_SUB, nb)
    tq = nsub * ATT_BLOCK
    lc = ak_ctx.shape[1]
    qspec = pl.BlockSpec((1, tq, BRANCH_W), lambda b, n: (b, n, 0))
    kx_spec = pl.BlockSpec((1, lc, LANES), lambda b, n: (b, 0, 0))
    vx_spec = pl.BlockSpec((1, LANES, lc), lambda b, n: (b, 0, 0))
    sink_spec = pl.BlockSpec((1, 1, ATT_Q_HEADS * ATT_BLOCK), lambda b, n: (layer, 0, 0))
    if local:
        prev = lambda n: jnp.maximum(n * nsub - 1, 0)
        nxt = lambda n: jnp.minimum((n + 1) * nsub, nb - 1)
        kspecs = [pl.BlockSpec((1, ATT_BLOCK, LANES), lambda b, n: (b, prev(n), 0)),
                  pl.BlockSpec((1, tq, LANES), lambda b, n: (b, n, 0)),
                  pl.BlockSpec((1, ATT_BLOCK, LANES), lambda b, n: (b, nxt(n), 0))]
        vspecs = [pl.BlockSpec((1, LANES, ATT_BLOCK), lambda b, n: (b, 0, prev(n))),
                  pl.BlockSpec((1, LANES, tq), lambda b, n: (b, 0, n)),
                  pl.BlockSpec((1, LANES, ATT_BLOCK), lambda b, n: (b, 0, nxt(n)))]
        in_specs = [qspec, *kspecs, kx_spec, *vspecs, vx_spec, sink_spec]
        args = (aq, ak, ak, ak, ak_ctx, avt, avt, avt, avt_ctx, sink)
    else:
        in_specs = [qspec, kx_spec, vx_spec, sink_spec]
        args = (aq, ak_ctx, avt_ctx, sink)
    return pl.pallas_call(
        functools.partial(_attn_kernel, nb=nb, local=local),
        grid=(bsz, nb // nsub),
        in_specs=in_specs,
        out_specs=pl.BlockSpec((1, BRANCH_W, tq), lambda b, n: (b, 0, n)),
        out_shape=jax.ShapeDtypeStruct((bsz, BRANCH_W, seq), BF16),
        compiler_params=_params("parallel", "parallel"),
        name="attention",
    )(*args)


def _merge_kernel(x_ref, mod_ref, g_ref, hf_ref, hb_ref, og_ref, yb_ref, att_ref, yd_ref, wg_ref, wb_ref, wo_ref,
                  o_ref):
    d = D_MODEL
    x = x_ref[0]
    u = (_rms(x) * g_ref[0, 0:1, :] * (1.0 + mod_ref[0, 0, 1:2, :]) + mod_ref[0, 0, 0:1, :]).astype(BF16)
    ya_t = (og_ref[0].astype(F32) * (hf_ref[0].astype(F32) + hb_ref[0].astype(F32))).astype(BF16)
    ys = (ya_t, yb_ref[0], att_ref[0], yd_ref[0])
    feature_major = (True, False, True, False)
    merged = None
    for n in range(N_BRANCH):
        gate = _sigmoid(_mm(u, wg_ref[0, :, n * d:(n + 1) * d]))
        proj = _tn(ys[n], wb_ref[0, n]) if feature_major[n] else _mm(ys[n], wb_ref[0, n])
        term = gate * proj
        merged = term if merged is None else merged + term
    out = _mm(merged.astype(BF16), wo_ref[0])
    o_ref[0] = x + mod_ref[0, 0, 2:3, :] * (_rms(out) * g_ref[0, 1:2, :])


def _merge(x, mod, norm_g, hf, hb, og, yb, att, yd, w_mg, w_branch, w_out, *, layer, mod_row, tm):
    bsz, seq, d = x.shape
    mrow = (lambda b: b) if mod_row is None else (lambda b: mod_row)
    tok = lambda w: pl.BlockSpec((1, tm, w), lambda b, i: (b, i, 0))
    feat = pl.BlockSpec((1, BRANCH_W, tm), lambda b, i: (b, 0, i))
    lsel3 = lambda b, i: (layer, 0, 0)
    return pl.pallas_call(
        _merge_kernel,
        grid=(bsz, seq // tm),
        in_specs=[tok(d),
                  pl.BlockSpec((1, 1, 6, d), lambda b, i: (layer, mrow(b), 0, 0)),
                  _const_spec((1, 4, d), lsel3),
                  feat, feat, feat, tok(BRANCH_W), feat, tok(BRANCH_W),
                  _const_spec((1, d, N_BRANCH * d), lsel3),
                  _const_spec((1, N_BRANCH, BRANCH_W, d), lambda b, i: (layer, 0, 0, 0)),
                  _const_spec((1, d, d), lsel3)],
        out_specs=tok(d),
        out_shape=jax.ShapeDtypeStruct((bsz, seq, d), F32),
        compiler_params=_params("parallel", "parallel"),
        name="merge",
    )(x, mod, norm_g, hf, hb, og, yb, att, yd, w_mg, w_branch, w_out)


def _ffn_kernel(xm_ref, xp_ref, xn_ref, mod_ref, g_ref, wu_ref, wc_ref, wd_ref, o_ref, u_scr, rg_scr, rv_scr,
                h_scr, *, tm, seq):
    u = _modulated_tile(xm_ref, xp_ref, xn_ref, g_ref[0, 2:3, :], mod_ref[0, 0, 3:4, :], mod_ref[0, 0, 4:5, :],
                        tm, seq)
    u_scr[...] = u.astype(BF16)
    cw = FF_CHUNK
    for c in range(D_FF // cw):
        gcols = slice(c * cw, (c + 1) * cw)
        vcols = slice(D_FF + c * cw, D_FF + (c + 1) * cw)
        rg_scr[...] = _mm(u_scr[...], wu_ref[0, :, gcols])
        rv_scr[...] = _mm(u_scr[...], wu_ref[0, :, vcols])
        half = _conv3_taps(rg_scr, 0.5 * wc_ref[0, :, gcols], tm)
        val = _conv3(rv_scr, wc_ref, vcols, tm)
        h_scr[:, gcols] = ((half + half * jnp.tanh(half)) * val).astype(BF16)
    out = _mm(h_scr[...], wd_ref[0])
    x = xm_ref[0]
    o_ref[0] = x + mod_ref[0, 0, 5:6, :] * (_rms(out) * g_ref[0, 3:4, :])


def _ffn(x, mod, norm_g, w_up, w_conv, w_down, *, layer, mod_row, tm):
    bsz, seq, d = x.shape
    main, prev, nxt = _halo_specs(tm, d, seq)
    mrow = (lambda b: b) if mod_row is None else (lambda b: mod_row)
    lsel3 = lambda b, i: (layer, 0, 0)
    return pl.pallas_call(
        functools.partial(_ffn_kernel, tm=tm, seq=seq),
        grid=(bsz, seq // tm),
        in_specs=[main, prev, nxt,
                  pl.BlockSpec((1, 1, 6, d), lambda b, i: (layer, mrow(b), 0, 0)),
                  _const_spec((1, 4, d), lsel3),
                  _const_spec((1, d, 2 * D_FF), lsel3),
                  _const_spec((1, 3, 2 * D_FF), lsel3),
                  _const_spec((1, D_FF, d), lsel3)],
        out_specs=main,
        out_shape=jax.ShapeDtypeStruct((bsz, seq, d), F32),
        scratch_shapes=[pltpu.VMEM((tm + 2 * HALO, d), BF16), pltpu.VMEM((tm + 2 * HALO, FF_CHUNK), F32),
                        pltpu.VMEM((tm + 2 * HALO, FF_CHUNK), F32), pltpu.VMEM((tm, D_FF), BF16)],
        compiler_params=_params("parallel", "parallel"),
        name="convffn",
    )(x, x, x, mod, norm_g, w_up, w_conv, w_down)


def _rope_tables(seq):
    t = jnp.arange(seq, dtype=jnp.int32)[:, None]
    lane = jnp.arange(LANES, dtype=jnp.int32)[None, :]
    in_head = lane % ATT_HEAD_DIM
    n_freq = ATT_HEAD_DIM // 4
    inv_freq = ROPE_BASE ** (-(in_head % n_freq).astype(F32) / n_freq)
    pos = jnp.where(in_head < ATT_HEAD_DIM // 2, t // GRID_W, t % GRID_W).astype(F32)
    ang = pos * inv_freq
    sign = jnp.where((in_head % (2 * n_freq)) < n_freq, -1.0, 1.0)
    return jnp.cos(ang), sign * jnp.sin(ang)


def _tile_rows(seq):
    return min(seq, 512)


def kernel(x, c, ctx, c_ctx, w_mod, b_mod, norm_g, w_in, mlstm_qk_conv, mlstm_gate_bias, pool_w, pool_scale,
           attn_sink, sconv_w, w_branch, w_out, ffn_up, ffn_conv, ffn_down):
    bsz, seq, d = x.shape
    lc = ctx.shape[1]
    depth = w_mod.shape[0]
    assert d == D_MODEL and seq % 512 == 0 and lc % 256 == 0 and bsz + 1 <= 8

    cond = jnp.concatenate([c, c_ctx[None, :], jnp.zeros((8 - bsz - 1, d), F32)], axis=0)
    mod = _modulation(cond, w_mod, b_mod).reshape(depth, 8, 6, d)

    g0 = BRANCH_W * 4
    g1 = g0 + 4 * MLSTM_HEADS
    mg0 = w_in.shape[-1] - N_BRANCH * d
    p0 = g1
    aq0 = p0 + BRANCH_W
    ak0 = aq0 + BRANCH_W
    av0 = ak0 + ATT_KV_HEADS * ATT_HEAD_DIM
    sc0 = av0 + ATT_KV_HEADS * ATT_HEAD_DIM
    w_aq = w_in[:, :, aq0:ak0].reshape(depth, d, ATT_KV_HEADS, ATT_Q_HEADS // ATT_KV_HEADS, ATT_HEAD_DIM)
    w_aq = jnp.swapaxes(w_aq, 2, 3).reshape(depth, d, BRANCH_W)
    v0 = 2 * BRANCH_W
    w1 = jnp.concatenate([w_in[:, :, :v0], w_in[:, :, p0:aq0], w_aq, w_in[:, :, ak0:av0], w_in[:, :, sc0:mg0]],
                         axis=-1).astype(BF16)
    perm = jnp.array(_GATE_PERM)
    wt = jnp.concatenate([w_in[:, :, v0:g0], w_in[:, :, av0:sc0], w_in[:, :, g0:g1][:, :, perm]], axis=-1)
    wt = jnp.swapaxes(wt, 1, 2).astype(BF16)
    gb = mlstm_gate_bias.reshape(depth, 4 * MLSTM_HEADS)[:, perm][:, :, None]
    w_mg = w_in[:, :, mg0:].astype(BF16)
    pw = pool_w.astype(BF16)
    ps = pool_scale[:, None, :]
    sink_heads = attn_sink.reshape(depth, ATT_KV_HEADS, ATT_Q_HEADS // ATT_KV_HEADS).swapaxes(1, 2)
    sink = jnp.broadcast_to(sink_heads.reshape(depth, 1, ATT_Q_HEADS, 1),
                            (depth, 1, ATT_Q_HEADS, ATT_BLOCK)).reshape(depth, 1, ATT_Q_HEADS * ATT_BLOCK)
    wb = w_branch.astype(BF16)
    wo = w_out.astype(BF16)
    wu = ffn_up.astype(BF16)
    wc = ffn_conv
    wd = ffn_down.astype(BF16)

    cos_x, sin_x = _rope_tables(seq)
    cos_c, sin_c = jnp.ones((lc, LANES), F32), jnp.zeros((lc, LANES), F32)

    hd = MLSTM_HD
    zero_c = jnp.zeros((bsz, MLSTM_HEADS, 2 * hd, hd), F32)
    zero_m = jnp.zeros((bsz, 8, LANES), F32)
    tm_x, tm_c = _tile_rows(seq), _tile_rows(lc)
    scan = functools.partial(_mlstm, chunk=MLSTM_CHUNK)

    h = ctx
    for l in range(depth):
        ctx_out = l < depth - 1
        proj = functools.partial(_inproj, mod=mod, norm_g=norm_g, w1=w1, wt=wt, gb=gb, qkc=mlstm_qk_conv,
                                 scw=sconv_w, pw=pw, ps=ps, layer=l)
        qk_c, v_c, og_c, gt_c, yb_c, aq_c, ak_c, avt_c, yd_c = proj(h, cos=cos_c, sin=sin_c, mod_row=bsz, tm=tm_c)
        qk_x, v_x, og_x, gt_x, yb_x, aq_x, ak_x, avt_x, yd_x = proj(x, cos=cos_x, sin=sin_x, mod_row=None, tm=tm_x)
        hf_c, cf, mf = scan(qk_c, v_c, gt_c, zero_c, zero_m, reverse=False)
        hb_c, cb, mb = scan(qk_c, v_c, gt_c, zero_c, zero_m, reverse=True)
        hf_x, _, _ = scan(qk_x, v_x, gt_x, cf, mf, reverse=False)
        hb_x, _, _ = scan(qk_x, v_x, gt_x, cb, mb, reverse=True)
        att_x = _attention(aq_x, ak_x, avt_x, ak_c, avt_c, sink, layer=l, local=True)
        tail = dict(norm_g=norm_g, layer=l)
        x = _merge(x, mod, hf=hf_x, hb=hb_x, og=og_x, yb=yb_x, att=att_x, yd=yd_x, w_mg=w_mg, w_branch=wb,
                   w_out=wo, mod_row=None, tm=tm_x, **tail)
        x = _ffn(x, mod, w_up=wu, w_conv=wc, w_down=wd, mod_row=None, tm=tm_x, **tail)
        if ctx_out:
            att_c = _attention(aq_c, ak_c, avt_c, ak_c, avt_c, sink, layer=l, local=False)
            h = _merge(h, mod, hf=hf_c, hb=hb_c, og=og_c, yb=yb_c, att=att_c, yd=yd_c, w_mg=w_mg, w_branch=wb,
                       w_out=wo, mod_row=bsz, tm=tm_c, **tail)
            h = _ffn(h, mod, w_up=wu, w_conv=wc, w_down=wd, mod_row=bsz, tm=tm_c, **tail)
    return x
```

```python
import functools

import jax
import jax.numpy as jnp
from jax import lax
from jax.experimental import pallas as pl
from jax.experimental.pallas import tpu as pltpu

F32 = jnp.float32
BF16 = jnp.bfloat16
HIGHEST = lax.Precision.HIGHEST

D_MODEL = 1024
GRID_W = 64
N_BRANCH = 4
BRANCH_W = D_MODEL // 2
MLSTM_HEADS = 4
MLSTM_HD = BRANCH_W // MLSTM_HEADS
MLSTM_CHUNK = 256
MLSTM_CHUNKS_PER_STEP = 8
POOL_HALF = (1, 2, 4, 8)
POOL_GW = BRANCH_W // len(POOL_HALF)
ATT_HEAD_DIM = 64
ATT_Q_HEADS = BRANCH_W // ATT_HEAD_DIM
ATT_KV_HEADS = 2
ATT_BLOCK = 128
ATT_SUB = 8
LOG2E = 1.4426950408889634
ATT_Q_SCALE = ATT_HEAD_DIM ** -0.5 * LOG2E
ROPE_BASE = 10000.0
D_FF = 2816
EPS = 1e-6

LANES = 128
HALO = 8
FF_CHUNK = 256
MERGE_PARTS = 2
VMEM_LIMIT = 56 * 1024 * 1024

_OFF = {}
_acc = 0
for _name, _w in (("q", 512), ("k", 512), ("pool", 512), ("aq", 512), ("ak", 128), ("sb", 512), ("sc", 512),
                  ("sx", 512)):
    _OFF[_name] = _acc
    _acc += _w
W1_COLS = _acc
_TOFF = {}
_acc = 0
for _name, _w in (("v", 512), ("o", 512), ("av", 128), ("gates", 16)):
    _TOFF[_name] = _acc
    _acc += _w
WT_ROWS = _acc
_GATE_PERM = (0, 1, 2, 3, 8, 9, 10, 11, 4, 5, 6, 7, 12, 13, 14, 15)


def _iota(shape, dim):
    return lax.broadcasted_iota(jnp.int32, shape, dim)


def _rms(x):
    return x * lax.rsqrt(jnp.mean(x * x, axis=-1, keepdims=True) + EPS)


def _sigmoid(x):
    return 1.0 / (1.0 + jnp.exp(-x))


def _nt(a, b, **kw):
    return lax.dot_general(a, b, (((1,), (1,)), ((), ())), preferred_element_type=F32, **kw)


def _tn(a, b, **kw):
    return lax.dot_general(a, b, (((0,), (0,)), ((), ())), preferred_element_type=F32, **kw)


def _mm(a, b, **kw):
    return jnp.dot(a, b, preferred_element_type=F32, **kw)


def _params(*sem, flags=None):
    return pltpu.CompilerParams(dimension_semantics=sem, vmem_limit_bytes=VMEM_LIMIT, flags=flags)


def _const_spec(shape, index_map):
    return pl.BlockSpec(shape, index_map, pipeline_mode=pl.Buffered(1))


def _mod_kernel(c_ref, w_ref, b_ref, o_ref):
    cond = c_ref[...]
    sc = cond * _sigmoid(cond)
    o_ref[0] = _mm(sc, w_ref[0], precision=HIGHEST) + b_ref[0]


def _modulation(cond, w_mod, b_mod):
    depth, d, n = w_mod.shape
    rows = cond.shape[0]
    tn = 1024
    return pl.pallas_call(
        _mod_kernel,
        grid=(depth, n // tn),
        in_specs=[pl.BlockSpec((rows, d), lambda l, j: (0, 0)),
                  pl.BlockSpec((1, d, tn), lambda l, j: (l, 0, j)),
                  pl.BlockSpec((1, 1, tn), lambda l, j: (l, 0, j))],
        out_specs=pl.BlockSpec((1, rows, tn), lambda l, j: (l, 0, j)),
        out_shape=jax.ShapeDtypeStruct((depth, rows, n), F32),
        compiler_params=_params("parallel", "parallel"),
        name="modulation",
    )(cond, w_mod, b_mod.reshape(depth, 1, n))


def _halo_specs(tm, d, seq):
    nh = seq // HALO
    per = tm // HALO
    main = pl.BlockSpec((1, tm, d), lambda b, i: (b, i, 0))
    prev = pl.BlockSpec((1, HALO, d), lambda b, i: (b, jnp.maximum(i * per - 1, 0), 0))
    nxt = pl.BlockSpec((1, HALO, d), lambda b, i: (b, jnp.minimum((i + 1) * per, nh - 1), 0))
    return main, prev, nxt


def _modulated_tile(xm_ref, xp_ref, xn_ref, gain, shift, scale, tm, seq):
    i = pl.program_id(1)
    mult = gain * (1.0 + scale)
    mod = lambda x: _rms(x) * mult + shift
    before = jnp.where(i > 0, mod(xp_ref[0]), 0.0)
    after = jnp.where((i + 1) * tm < seq, mod(xn_ref[0]), 0.0)
    return jnp.concatenate([before, mod(xm_ref[0]), after], axis=0)


def _conv3_taps(r_ref, w, tm):
    return (r_ref[HALO - 1:HALO - 1 + tm, :] * w[0:1] + r_ref[HALO:HALO + tm, :] * w[1:2]
            + r_ref[HALO + 1:HALO + 1 + tm, :] * w[2:3])


def _conv3(r_ref, w_ref, cols, tm):
    return _conv3_taps(r_ref, w_ref[0, :, cols], tm)


def _rope(r, cos, sin):
    lane = _iota((1, LANES), 1)
    first = (lane % 32) < 16
    swapped = jnp.where(first, pltpu.roll(r, LANES - 16, 1), pltpu.roll(r, 16, 1))
    return r * cos + swapped * sin


def _inproj_kernel(xm_ref, xp_ref, xn_ref, mod_ref, g_ref, w_ref, wt_ref, gb_ref, qkc_ref, scw_ref, pw_ref,
                   ps_ref, cos_ref, sin_ref,
                   qk_out, vt_out, ogt_out, gates_out, yb_out, aq_out, ak_out, avt_out, yd_out,
                   u_scr, um_scr, r_scr, *, tm, seq):
    i = pl.program_id(1)
    u = _modulated_tile(xm_ref, xp_ref, xn_ref, g_ref[0, 0:1, :], mod_ref[0, 0, 0:1, :], mod_ref[0, 0, 1:2, :],
                        tm, seq)
    u_scr[...] = u.astype(BF16)
    um_scr[...] = u[HALO:HALO + tm].astype(BF16)

    def wcols(name, width, extra=0):
        o = _OFF[name] + extra
        return w_ref[0, :, o:o + width]

    for c, name in enumerate(("q", "k")):
        cols = slice(c * BRANCH_W, (c + 1) * BRANCH_W)
        r_scr[...] = _mm(u_scr[...], wcols(name, BRANCH_W))
        half = _conv3_taps(r_scr, 0.5 * qkc_ref[0, :, cols], tm)
        a = half + half * jnp.tanh(half)
        if name == "k":
            a = a * MLSTM_HD ** -0.5
        qk_out[0, :, cols] = a.astype(BF16)

    um = um_scr[...]

    def trows(name, height):
        o = _TOFF[name]
        return wt_ref[0, o:o + height, :]

    vt_out[0] = _nt(trows("v", BRANCH_W), um).astype(BF16)
    ogt_out[0] = _sigmoid(_nt(trows("o", BRANCH_W), um)).astype(BF16)
    avt_out[0] = _nt(trows("av", LANES), um).astype(BF16)

    g = _nt(trows("gates", 16), um) + gb_ref[0]
    logsig = jnp.minimum(g, 0.0) - jnp.log(1.0 + jnp.exp(-jnp.abs(g)))
    is_forget = (_iota((16, 1), 0) % 8) >= MLSTM_HEADS
    g = jnp.where(is_forget, logsig, g) * LOG2E
    gates_out[0, 0] = g[0:8]
    gates_out[0, 1] = g[8:16]

    r_scr[...] = _mm(u_scr[...], wcols("pool", BRANCH_W))
    assert max(POOL_HALF) <= HALO
    t_top = i * tm + _iota((HALO, 1), 0)
    t_bot = t_top + (tm - HALO)
    for gi, half in enumerate(POOL_HALF):
        cols = slice(gi * POOL_GW, (gi + 1) * POOL_GW)
        acc = r_scr[HALO - half:HALO - half + tm, cols]
        for o in range(-half + 1, half):
            acc = acc + r_scr[HALO + o:HALO + o + tm, cols]
        inv = lambda t: 1.0 / (jnp.minimum(t + half, seq) - jnp.maximum(t - half, 0)).astype(F32)
        mean = jnp.concatenate([acc[:HALO] * inv(t_top), acc[HALO:tm - HALO] * (0.5 / half),
                                acc[tm - HALO:] * inv(t_bot)], axis=0)
        delta = mean - r_scr[HALO:HALO + tm, cols]
        y = _mm(delta.astype(BF16), pw_ref[0, gi]) * ps_ref[0, 0:1, cols]
        yb_out[0, :, cols] = y.astype(BF16)

    cos = cos_ref[...]
    sin = sin_ref[...]
    rq = _mm(um, wcols("aq", BRANCH_W))
    for j in range(BRANCH_W // LANES):
        cols = slice(j * LANES, (j + 1) * LANES)
        aq_out[0, :, cols] = (_rope(rq[:, cols], cos, sin) * ATT_Q_SCALE).astype(BF16)
    ak_out[0] = _rope(_mm(um, wcols("ak", LANES)), cos, sin).astype(BF16)

    rb = _mm(um, wcols("sb", BRANCH_W))
    r_scr[...] = _mm(u_scr[...], wcols("sc", BRANCH_W)) * _mm(u_scr[...], wcols("sx", BRANCH_W))
    yd_out[0] = (rb * _conv3(r_scr, scw_ref, slice(0, BRANCH_W), tm)).astype(BF16)


def _inproj(x, mod, norm_g, w1, wt, gb, qkc, scw, pw, ps, cos, sin, *, layer, mod_row, tm):
    bsz, seq, d = x.shape
    nt = seq // tm
    main, prev, nxt = _halo_specs(tm, d, seq)
    mrow = (lambda b: b) if mod_row is None else (lambda b: mod_row)
    lsel3 = lambda b, i: (layer, 0, 0)
    in_specs = [
        main, prev, nxt,
        pl.BlockSpec((1, 1, 6, d), lambda b, i: (layer, mrow(b), 0, 0)),
        _const_spec((1, 4, d), lsel3),
        _const_spec((1, d, W1_COLS), lsel3),
        _const_spec((1, WT_ROWS, d), lsel3),
        _const_spec((1, 16, 1), lsel3),
        _const_spec((1, 3, 2 * BRANCH_W), lsel3),
        _const_spec((1, 3, BRANCH_W), lsel3),
        _const_spec((1, len(POOL_HALF), POOL_GW, POOL_GW), lambda b, i: (layer, 0, 0, 0)),
        _const_spec((1, 1, BRANCH_W), lsel3),
        pl.BlockSpec((tm, LANES), lambda b, i: (i, 0)),
        pl.BlockSpec((tm, LANES), lambda b, i: (i, 0)),
    ]
    tok = lambda w: pl.BlockSpec((1, tm, w), lambda b, i: (b, i, 0))
    feat = lambda w: pl.BlockSpec((1, w, tm), lambda b, i: (b, 0, i))
    out_specs = [tok(2 * BRANCH_W), feat(BRANCH_W), feat(BRANCH_W),
                 pl.BlockSpec((1, 2, 8, tm), lambda b, i: (b, 0, 0, i)),
                 tok(BRANCH_W), tok(BRANCH_W), tok(LANES), feat(LANES), tok(BRANCH_W)]
    act = lambda w: jax.ShapeDtypeStruct((bsz, seq, w), BF16)
    act_t = lambda w: jax.ShapeDtypeStruct((bsz, w, seq), BF16)
    out_shape = [act(2 * BRANCH_W), act_t(BRANCH_W), act_t(BRANCH_W),
                 jax.ShapeDtypeStruct((bsz, 2, 8, seq), F32),
                 act(BRANCH_W), act(BRANCH_W), act(LANES), act_t(LANES), act(BRANCH_W)]
    return pl.pallas_call(
        functools.partial(_inproj_kernel, tm=tm, seq=seq),
        grid=(bsz, nt),
        in_specs=in_specs,
        out_specs=out_specs,
        out_shape=out_shape,
        scratch_shapes=[pltpu.VMEM((tm + 2 * HALO, d), BF16), pltpu.VMEM((tm, d), BF16),
                        pltpu.VMEM((tm + 2 * HALO, BRANCH_W), F32)],
        compiler_params=_params("parallel", "parallel"),
        name="inproj",
    )(x, x, x, mod, norm_g, w1, wt, gb, qkc, scw, pw, ps, cos, sin)


def _split3(x):
    x = jnp.concatenate([x, jnp.zeros_like(x)], axis=0)
    hi = x.astype(BF16)
    r1 = x - hi.astype(F32)
    mid = r1.astype(BF16)
    lo = (r1 - mid.astype(F32)).astype(BF16)
    return jnp.concatenate([hi, mid, lo], axis=0)


def _sum3(y):
    return (y[0:8] + y[16:24]) + y[32:40]


def _mlstm_kernel(qk_ref, vt_ref, g_ref, c0_ref, m0_ref, h_out, c_out, m_out, *, chunk, reverse):
    L = chunk
    hd = MLSTM_HD
    nck = qk_ref.shape[1] // L

    @pl.when(pl.program_id(1) == 0)
    def _():
        c_out[...] = c0_ref[...]
        m_out[...] = m0_ref[...]

    s_idx = _iota((L, L), 0)
    t_idx = _iota((L, L), 1)
    seen = (s_idx >= t_idx) if reverse else (s_idx <= t_idx)
    bias = jnp.where(seen, 0.0, -jnp.inf)
    seen_bf = seen.astype(BF16)
    is_last = _iota((1, L), 1) == (0 if reverse else L - 1)
    ones = jnp.ones((hd, L), BF16)
    neg_inf = -jnp.inf

    heads = range(MLSTM_HEADS)
    wide = lambda parts: jnp.concatenate(parts, axis=1)
    toks = [slice(((nck - 1 - i) if reverse else i) * L, (((nck - 1 - i) if reverse else i) + 1) * L)
            for i in range(nck)]

    qs, ks, vaugs, b_rows, ig_rows, keys, scores = [], [], [], [], [], [], []
    for tok in toks:
        g = g_ref[0, 0, :, tok]
        g3 = _split3(g)
        cum_row = _sum3(_mm(g3, seen_bf))
        key_rows = cum_row[MLSTM_HEADS:] - g[:MLSTM_HEADS]
        key_cols = jnp.transpose(jnp.concatenate([key_rows, jnp.zeros_like(key_rows)], axis=0))
        for h in heads:
            qs.append(qk_ref[0, tok, h * hd:(h + 1) * hd])
            ks.append(qk_ref[0, tok, BRANCH_W + h * hd:BRANCH_W + (h + 1) * hd])
            vaugs.append(jnp.concatenate([vt_ref[0, h * hd:(h + 1) * hd, tok], ones], axis=0))
            b_rows.append(cum_row[MLSTM_HEADS + h:MLSTM_HEADS + h + 1, :])
            ig_rows.append(g[h:h + 1, :])
            keys.append(jnp.broadcast_to(key_cols[:, h:h + 1], (L, L)))
            scores.append(_nt(ks[-1], qs[-1]))
    n_pairs = len(qs)
    b_row = wide(b_rows)
    dmat = (b_row - wide(keys)) + wide([bias] * n_pairs)
    m_intra = jnp.max(dmat, axis=0, keepdims=True)
    p = (wide(scores) * jnp.exp2(dmat - m_intra)).astype(BF16)
    b_tots = [jnp.max(jnp.where(is_last, b, neg_inf), axis=1, keepdims=True) for b in b_rows]
    u_loc = wide([bt - b + ig for bt, b, ig in zip(b_tots, b_rows, ig_rows)])
    m_locs = [jnp.max(u_loc[:, e * L:(e + 1) * L], axis=1, keepdims=True) for e in range(n_pairs)]
    wk_loc = jnp.exp2(u_loc - wide([jnp.broadcast_to(m, (1, L)) for m in m_locs]))
    r_intra = [_mm(vaugs[e], p[:, e * L:(e + 1) * L]) for e in range(n_pairs)]
    kv_loc = [_mm((vaugs[e].astype(F32) * wk_loc[:, e * L:(e + 1) * L]).astype(BF16), ks[e])
              for e in range(n_pairs)]

    for i, tok in enumerate(toks):
        for h in heads:
            e = i * MLSTM_HEADS + h
            lanes = slice(e * L, (e + 1) * L)
            m_prev = m_out[0, h:h + 1, 0:1]
            c_prev = c_out[0, h]
            a = b_rows[e] + m_prev
            mt = jnp.maximum(a, m_intra[:, lanes])
            r = (jnp.exp2(m_intra[:, lanes] - mt) * r_intra[e]
                 + jnp.exp2(a - mt) * _nt(c_prev.astype(BF16), qs[e]))
            den = jnp.maximum(jnp.abs(r[hd:]), jnp.exp2(-mt))
            h_out[0, h * hd:(h + 1) * hd, tok] = (r[:hd] / den).astype(BF16)
            m_new = jnp.max(jnp.where(is_last, mt, neg_inf), axis=1, keepdims=True)
            c_out[0, h] = (jnp.exp2(b_tots[e] + m_prev - m_new) * c_prev
                           + jnp.exp2(m_locs[e] - m_new) * kv_loc[e])
            m_out[0, h:h + 1, :] = jnp.broadcast_to(m_new, (1, LANES))


def _mlstm(qk, vt, gates, c0, m0, *, chunk, reverse):
    bsz, seq, _ = qk.shape
    nck = min(MLSTM_CHUNKS_PER_STEP, seq // chunk)
    tb = nck * chunk
    ns = seq // tb
    hd = MLSTM_HD
    direction = 1 if reverse else 0
    blk = (lambda j: ns - 1 - j) if reverse else (lambda j: j)
    state_c = pl.BlockSpec((1, MLSTM_HEADS, 2 * hd, hd), lambda b, j: (b, 0, 0, 0))
    state_m = pl.BlockSpec((1, 8, LANES), lambda b, j: (b, 0, 0))
    return pl.pallas_call(
        functools.partial(_mlstm_kernel, chunk=chunk, reverse=reverse),
        grid=(bsz, ns),
        in_specs=[pl.BlockSpec((1, tb, 2 * BRANCH_W), lambda b, j: (b, blk(j), 0)),
                  pl.BlockSpec((1, BRANCH_W, tb), lambda b, j: (b, 0, blk(j))),
                  pl.BlockSpec((1, 1, 8, tb), lambda b, j: (b, direction, 0, blk(j))),
                  state_c, state_m],
        out_specs=[pl.BlockSpec((1, BRANCH_W, tb), lambda b, j: (b, 0, blk(j))),
                   state_c, state_m],
        out_shape=[jax.ShapeDtypeStruct((bsz, BRANCH_W, seq), BF16),
                   jax.ShapeDtypeStruct((bsz, MLSTM_HEADS, 2 * hd, hd), F32),
                   jax.ShapeDtypeStruct((bsz, 8, LANES), F32)],
        compiler_params=_params("parallel", "arbitrary"),
        name="mlstm_bwd" if reverse else "mlstm_fwd",
    )(qk, vt, gates, c0, m0)


def _attn_kernel(*refs, nb, local):
    blk = ATT_BLOCK
    hd = ATT_HEAD_DIM
    if local:
        q_ref, kp_ref, kc_ref, kn_ref, kx_ref, vp_ref, vc_ref, vn_ref, vx_ref, sink_ref, o_ref = refs
        k_loc = jnp.concatenate([kp_ref[0], kc_ref[0], kn_ref[0]], axis=0)
        vt_loc = jnp.concatenate([vp_ref[0], vc_ref[0], vn_ref[0]], axis=1)
    else:
        q_ref, kx_ref, vx_ref, sink_ref, o_ref = refs
    nsub = q_ref.shape[1] // blk
    step = pl.program_id(1)
    lo = _iota((1, LANES), 1) < hd
    nq = ATT_Q_HEADS * blk
    sk = sink_ref[0] * LOG2E
    if local:
        kj = _iota((blk, nq), 0)
        qi = _iota((blk, nq), 1) % blk
        after = kj >= qi
        before = kj <= qi
        band_prev = jnp.where(after, 0.0, -jnp.inf)
        band_next = jnp.where(before, 0.0, -jnp.inf)

    scores, values, prev_bias, next_bias = [], [], [], []
    for i in range(nsub):
        if local:
            k = jnp.concatenate([k_loc[i * blk:(i + 3) * blk], kx_ref[0]], axis=0)
            vt = jnp.concatenate([vt_loc[:, i * blk:(i + 3) * blk], vx_ref[0]], axis=1)
            n = step * nsub + i
            prev_bias.append(band_prev if i > 0 else jnp.where(after & (n > 0), 0.0, -jnp.inf))
            next_bias.append(band_next if i < nsub - 1 else jnp.where(before & (n < nb - 1), 0.0, -jnp.inf))
        else:
            k = kx_ref[0]
            vt = vx_ref[0]
        nk = k.shape[0]
        values.append(jnp.concatenate([vt, jnp.ones((16, nk), BF16)], axis=0))
        parts = []
        for j in range(BRANCH_W // LANES):
            q2 = q_ref[0, i * blk:(i + 1) * blk, j * LANES:(j + 1) * LANES]
            zero = jnp.zeros_like(q2)
            parts += [jnp.where(lo, q2, zero), jnp.where(lo, zero, q2)]
        scores.append(_nt(k, jnp.concatenate(parts, axis=0)))
    s = jnp.concatenate(scores, axis=1)
    if local:
        s = jnp.concatenate([s[0:blk] + jnp.concatenate(prev_bias, axis=1), s[blk:2 * blk],
                             s[2 * blk:3 * blk] + jnp.concatenate(next_bias, axis=1), s[3 * blk:]], axis=0)
    sk = jnp.concatenate([sk] * nsub, axis=1)
    m = jnp.maximum(jnp.max(s, axis=0, keepdims=True), sk)
    e = jnp.exp2(s - m).astype(BF16)
    extra = jnp.exp2(sk - m)
    for i in range(nsub):
        r = _mm(values[i], e[:, i * nq:(i + 1) * nq])
        den = r[2 * hd:2 * hd + 1, :] + extra[:, i * nq:(i + 1) * nq]
        for p in range(ATT_Q_HEADS):
            c = p % 2
            head = p // 2 + 4 * c
            cols = slice(p * blk, (p + 1) * blk)
            o_ref[0, head * hd:(head + 1) * hd, i * blk:(i + 1) * blk] = (
                r[c * hd:(c + 1) * hd, cols] / den[:, cols]).astype(BF16)


def _attention(aq, ak, avt, ak_ctx, avt_ctx, sink, *, layer, local):
    bsz, seq, _ = aq.shape
    nb = seq // ATT_BLOCK
    nsub = min(ATT_SUB, nb)
    tq = nsub * ATT_BLOCK
    lc = ak_ctx.shape[1]
    qspec = pl.BlockSpec((1, tq, BRANCH_W), lambda b, n: (b, n, 0))
    kx_spec = pl.BlockSpec((1, lc, LANES), lambda b, n: (b, 0, 0))
    vx_spec = pl.BlockSpec((1, LANES, lc), lambda b, n: (b, 0, 0))
    sink_spec = pl.BlockSpec((1, 1, ATT_Q_HEADS * ATT_BLOCK), lambda b, n: (layer, 0, 0))
    if local:
        prev = lambda n: jnp.maximum(n * nsub - 1, 0)
        nxt = lambda n: jnp.minimum((n + 1) * nsub, nb - 1)
        kspecs = [pl.BlockSpec((1, ATT_BLOCK, LANES), lambda b, n: (b, prev(n), 0)),
                  pl.BlockSpec((1, tq, LANES), lambda b, n: (b, n, 0)),
                  pl.BlockSpec((1, ATT_BLOCK, LANES), lambda b, n: (b, nxt(n), 0))]
        vspecs = [pl.BlockSpec((1, LANES, ATT_BLOCK), lambda b, n: (b, 0, prev(n))),
                  pl.BlockSpec((1, LANES, tq), lambda b, n: (b, 0, n)),
                  pl.BlockSpec((1, LANES, ATT_BLOCK), lambda b, n: (b, 0, nxt(n)))]
        in_specs = [qspec, *kspecs, kx_spec, *vspecs, vx_spec, sink_spec]
        args = (aq, ak, ak, ak, ak_ctx, avt, avt, avt, avt_ctx, sink)
    else:
        in_specs = [qspec, kx_spec, vx_spec, sink_spec]
        args = (aq, ak_ctx, avt_ctx, sink)
    return pl.pallas_call(
        functools.partial(_attn_kernel, nb=nb, local=local),
        grid=(bsz, nb // nsub),
        in_specs=in_specs,
        out_specs=pl.BlockSpec((1, BRANCH_W, tq), lambda b, n: (b, 0, n)),
        out_shape=jax.ShapeDtypeStruct((bsz, BRANCH_W, seq), BF16),
        compiler_params=_params("parallel", "parallel"),
        name="attention",
    )(*args)


def _merge_kernel(x_ref, mod_ref, g_ref, hf_ref, hb_ref, og_ref, yb_ref, att_ref, yd_ref, wg_ref, wb_ref, wo_ref,
                  o_ref):
    d = D_MODEL
    tm = x_ref.shape[1]
    parts = MERGE_PARTS if tm % (MERGE_PARTS * 2 * LANES) == 0 else 1
    rows = tm // parts
    for part in range(parts):
        tok = slice(part * rows, (part + 1) * rows)
        x = x_ref[0, tok, :]
        u = (_rms(x) * g_ref[0, 0:1, :] * (1.0 + mod_ref[0, 0, 1:2, :]) + mod_ref[0, 0, 0:1, :]).astype(BF16)
        ya_t = (og_ref[0, :, tok].astype(F32)
                * (hf_ref[0, :, tok].astype(F32) + hb_ref[0, :, tok].astype(F32))).astype(BF16)
        ys = (ya_t, yb_ref[0, tok, :], att_ref[0, :, tok], yd_ref[0, tok, :])
        feature_major = (True, False, True, False)
        merged = None
        for n in range(N_BRANCH):
            gate = _sigmoid(_mm(u, wg_ref[0, :, n * d:(n + 1) * d]))
            proj = _tn(ys[n], wb_ref[0, n]) if feature_major[n] else _mm(ys[n], wb_ref[0, n])
            term = gate * proj
            merged = term if merged is None else merged + term
        out = _mm(merged.astype(BF16), wo_ref[0])
        o_ref[0, tok, :] = x + mod_ref[0, 0, 2:3, :] * (_rms(out) * g_ref[0, 1:2, :])


def _merge(x, mod, norm_g, hf, hb, og, yb, att, yd, w_mg, w_branch, w_out, *, layer, mod_row, tm):
    bsz, seq, d = x.shape
    mrow = (lambda b: b) if mod_row is None else (lambda b: mod_row)
    tok = lambda w: pl.BlockSpec((1, tm, w), lambda b, i: (b, i, 0))
    feat = pl.BlockSpec((1, BRANCH_W, tm), lambda b, i: (b, 0, i))
    lsel3 = lambda b, i: (layer, 0, 0)
    return pl.pallas_call(
        _merge_kernel,
        grid=(bsz, seq // tm),
        in_specs=[tok(d),
                  pl.BlockSpec((1, 1, 6, d), lambda b, i: (layer, mrow(b), 0, 0)),
                  _const_spec((1, 4, d), lsel3),
                  feat, feat, feat, tok(BRANCH_W), feat, tok(BRANCH_W),
                  _const_spec((1, d, N_BRANCH * d), lsel3),
                  _const_spec((1, N_BRANCH, BRANCH_W, d), lambda b, i: (layer, 0, 0, 0)),
                  _const_spec((1, d, d), lsel3)],
        out_specs=tok(d),
        out_shape=jax.ShapeDtypeStruct((bsz, seq, d), F32),
        compiler_params=_params("parallel", "parallel"),
        name="merge",
    )(x, mod, norm_g, hf, hb, og, yb, att, yd, w_mg, w_branch, w_out)


def _ffn_kernel(xm_ref, xp_ref, xn_ref, mod_ref, g_ref, wu_ref, wc_ref, wd_ref, o_ref, u_scr, rg_scr, rv_scr,
                h_scr, *, tm, seq):
    u = _modulated_tile(xm_ref, xp_ref, xn_ref, g_ref[0, 2:3, :], mod_ref[0, 0, 3:4, :], mod_ref[0, 0, 4:5, :],
                        tm, seq)
    u_scr[...] = u.astype(BF16)
    cw = FF_CHUNK
    for c in range(D_FF // cw):
        gcols = slice(c * cw, (c + 1) * cw)
        vcols = slice(D_FF + c * cw, D_FF + (c + 1) * cw)
        rg_scr[...] = _mm(u_scr[...], wu_ref[0, :, gcols])
        rv_scr[...] = _mm(u_scr[...], wu_ref[0, :, vcols])
        half = _conv3_taps(rg_scr, 0.5 * wc_ref[0, :, gcols], tm)
        val = _conv3(rv_scr, wc_ref, vcols, tm)
        h_scr[:, gcols] = ((half + half * jnp.tanh(half)) * val).astype(BF16)
    out = _mm(h_scr[...], wd_ref[0])
    x = xm_ref[0]
    o_ref[0] = x + mod_ref[0, 0, 5:6, :] * (_rms(out) * g_ref[0, 3:4, :])


def _ffn(x, mod, norm_g, w_up, w_conv, w_down, *, layer, mod_row, tm):
    bsz, seq, d = x.shape
    main, prev, nxt = _halo_specs(tm, d, seq)
    mrow = (lambda b: b) if mod_row is None else (lambda b: mod_row)
    lsel3 = lambda b, i: (layer, 0, 0)
    return pl.pallas_call(
        functools.partial(_ffn_kernel, tm=tm, seq=seq),
        grid=(bsz, seq // tm),
        in_specs=[main, prev, nxt,
                  pl.BlockSpec((1, 1, 6, d), lambda b, i: (layer, mrow(b), 0, 0)),
                  _const_spec((1, 4, d), lsel3),
                  _const_spec((1, d, 2 * D_FF), lsel3),
                  _const_spec((1, 3, 2 * D_FF), lsel3),
                  _const_spec((1, D_FF, d), lsel3)],
        out_specs=main,
        out_shape=jax.ShapeDtypeStruct((bsz, seq, d), F32),
        scratch_shapes=[pltpu.VMEM((tm + 2 * HALO, d), BF16), pltpu.VMEM((tm + 2 * HALO, FF_CHUNK), F32),
                        pltpu.VMEM((tm + 2 * HALO, FF_CHUNK), F32), pltpu.VMEM((tm, D_FF), BF16)],
        compiler_params=_params("parallel", "parallel"),
        name="convffn",
    )(x, x, x, mod, norm_g, w_up, w_conv, w_down)


def _rope_tables(seq):
    t = jnp.arange(seq, dtype=jnp.int32)[:, None]
    lane = jnp.arange(LANES, dtype=jnp.int32)[None, :]
    in_head = lane % ATT_HEAD_DIM
    n_freq = ATT_HEAD_DIM // 4
    inv_freq = ROPE_BASE ** (-(in_head % n_freq).astype(F32) / n_freq)
    pos = jnp.where(in_head < ATT_HEAD_DIM // 2, t // GRID_W, t % GRID_W).astype(F32)
    ang = pos * inv_freq
    sign = jnp.where((in_head % (2 * n_freq)) < n_freq, -1.0, 1.0)
    return jnp.cos(ang), sign * jnp.sin(ang)


def _tile_rows(seq):
    return min(seq, 512)


def kernel(x, c, ctx, c_ctx, w_mod, b_mod, norm_g, w_in, mlstm_qk_conv, mlstm_gate_bias, pool_w, pool_scale,
           attn_sink, sconv_w, w_branch, w_out, ffn_up, ffn_conv, ffn_down):
    bsz, seq, d = x.shape
    lc = ctx.shape[1]
    depth = w_mod.shape[0]
    assert d == D_MODEL and seq % 512 == 0 and lc % 256 == 0 and bsz + 1 <= 8

    cond = jnp.concatenate([c, c_ctx[None, :], jnp.zeros((8 - bsz - 1, d), F32)], axis=0)
    mod = _modulation(cond, w_mod, b_mod).reshape(depth, 8, 6, d)

    g0 = BRANCH_W * 4
    g1 = g0 + 4 * MLSTM_HEADS
    mg0 = w_in.shape[-1] - N_BRANCH * d
    p0 = g1
    aq0 = p0 + BRANCH_W
    ak0 = aq0 + BRANCH_W
    av0 = ak0 + ATT_KV_HEADS * ATT_HEAD_DIM
    sc0 = av0 + ATT_KV_HEADS * ATT_HEAD_DIM
    w_aq = w_in[:, :, aq0:ak0].reshape(depth, d, ATT_KV_HEADS, ATT_Q_HEADS // ATT_KV_HEADS, ATT_HEAD_DIM)
    w_aq = jnp.swapaxes(w_aq, 2, 3).reshape(depth, d, BRANCH_W)
    v0 = 2 * BRANCH_W
    w1 = jnp.concatenate([w_in[:, :, :v0], w_in[:, :, p0:aq0], w_aq, w_in[:, :, ak0:av0], w_in[:, :, sc0:mg0]],
                         axis=-1).astype(BF16)
    perm = jnp.array(_GATE_PERM)
    wt = jnp.concatenate([w_in[:, :, v0:g0], w_in[:, :, av0:sc0], w_in[:, :, g0:g1][:, :, perm]], axis=-1)
    wt = jnp.swapaxes(wt, 1, 2).astype(BF16)
    gb = mlstm_gate_bias.reshape(depth, 4 * MLSTM_HEADS)[:, perm][:, :, None]
    w_mg = w_in[:, :, mg0:].astype(BF16)
    pw = pool_w.astype(BF16)
    ps = pool_scale[:, None, :]
    sink_heads = attn_sink.reshape(depth, ATT_KV_HEADS, ATT_Q_HEADS // ATT_KV_HEADS).swapaxes(1, 2)
    sink = jnp.broadcast_to(sink_heads.reshape(depth, 1, ATT_Q_HEADS, 1),
                            (depth, 1, ATT_Q_HEADS, ATT_BLOCK)).reshape(depth, 1, ATT_Q_HEADS * ATT_BLOCK)
    wb = w_branch.astype(BF16)
    wo = w_out.astype(BF16)
    wu = ffn_up.astype(BF16)
    wc = ffn_conv
    wd = ffn_down.astype(BF16)

    cos_x, sin_x = _rope_tables(seq)
    cos_c, sin_c = jnp.ones((lc, LANES), F32), jnp.zeros((lc, LANES), F32)

    hd = MLSTM_HD
    zero_c = jnp.zeros((bsz, MLSTM_HEADS, 2 * hd, hd), F32)
    zero_m = jnp.zeros((bsz, 8, LANES), F32)
    tm_x, tm_c = _tile_rows(seq), _tile_rows(lc)
    scan = functools.partial(_mlstm, chunk=MLSTM_CHUNK)

    h = ctx
    for l in range(depth):
        ctx_out = l < depth - 1
        proj = functools.partial(_inproj, mod=mod, norm_g=norm_g, w1=w1, wt=wt, gb=gb, qkc=mlstm_qk_conv,
                                 scw=sconv_w, pw=pw, ps=ps, layer=l)
        qk_c, v_c, og_c, gt_c, yb_c, aq_c, ak_c, avt_c, yd_c = proj(h, cos=cos_c, sin=sin_c, mod_row=bsz, tm=tm_c)
        qk_x, v_x, og_x, gt_x, yb_x, aq_x, ak_x, avt_x, yd_x = proj(x, cos=cos_x, sin=sin_x, mod_row=None, tm=tm_x)
        hf_c, cf, mf = scan(qk_c, v_c, gt_c, zero_c, zero_m, reverse=False)
        hb_c, cb, mb = scan(qk_c, v_c, gt_c, zero_c, zero_m, reverse=True)
        hf_x, _, _ = scan(qk_x, v_x, gt_x, cf, mf, reverse=False)
        hb_x, _, _ = scan(qk_x, v_x, gt_x, cb, mb, reverse=True)
        att_x = _attention(aq_x, ak_x, avt_x, ak_c, avt_c, sink, layer=l, local=True)
        tail = dict(norm_g=norm_g, layer=l)
        x = _merge(x, mod, hf=hf_x, hb=hb_x, og=og_x, yb=yb_x, att=att_x, yd=yd_x, w_mg=w_mg, w_branch=wb,
                   w_out=wo, mod_row=None, tm=tm_x, **tail)
        x = _ffn(x, mod, w_up=wu, w_conv=wc, w_down=wd, mod_row=None, tm=tm_x, **tail)
        if ctx_out:
            att_c = _attention(aq_c, ak_c, avt_c, ak_c, avt_c, sink, layer=l, local=False)
            h = _merge(h, mod, hf=hf_c, hb=hb_c, og=og_c, yb=yb_c, att=att_c, yd=yd_c, w_mg=w_mg, w_branch=wb,
                       w_out=wo, mod_row=bsz, tm=tm_c, **tail)
            h = _ffn(h, mod, w_up=wu, w_conv=wc, w_down=wd, mod_row=bsz, tm=tm_c, **tail)
    return x
```

```python
import functools

import jax
import jax.numpy as jnp
from jax import lax
from jax.experimental import pallas as pl
from jax.experimental.pallas import tpu as pltpu

F32 = jnp.float32
BF16 = jnp.bfloat16
HIGHEST = lax.Precision.HIGHEST

D_MODEL = 1024
GRID_W = 64
N_BRANCH = 4
BRANCH_W = D_MODEL // 2
MLSTM_HEADS = 4
MLSTM_HD = BRANCH_W // MLSTM_HEADS
MLSTM_CHUNK = 256
MLSTM_CHUNKS_PER_STEP = 8
POOL_HALF = (1, 2, 4, 8)
POOL_GW = BRANCH_W // len(POOL_HALF)
ATT_HEAD_DIM = 64
ATT_Q_HEADS = BRANCH_W // ATT_HEAD_DIM
ATT_KV_HEADS = 2
ATT_BLOCK = 128
ATT_SUB = 8
LOG2E = 1.4426950408889634
ATT_Q_SCALE = ATT_HEAD_DIM ** -0.5 * LOG2E
ROPE_BASE = 10000.0
D_FF = 2816
EPS = 1e-6

LANES = 128
HALO = 8
FF_CHUNK = 256
MERGE_PARTS = 2
VMEM_LIMIT = 56 * 1024 * 1024

_OFF = {}
_acc = 0
for _name, _w in (("q", 512), ("k", 512), ("pool", 512), ("aq", 512), ("ak", 128), ("sb", 512), ("sc", 512),
                  ("sx", 512)):
    _OFF[_name] = _acc
    _acc += _w
W1_COLS = _acc
_TOFF = {}
_acc = 0
for _name, _w in (("v", 512), ("o", 512), ("av", 128), ("gates", 16)):
    _TOFF[_name] = _acc
    _acc += _w
WT_ROWS = _acc
_GATE_PERM = (0, 1, 2, 3, 8, 9, 10, 11, 4, 5, 6, 7, 12, 13, 14, 15)


def _iota(shape, dim):
    return lax.broadcasted_iota(jnp.int32, shape, dim)


def _rms(x):
    return x * lax.rsqrt(jnp.mean(x * x, axis=-1, keepdims=True) + EPS)


def _sigmoid(x):
    return 1.0 / (1.0 + jnp.exp(-x))


def _nt(a, b, **kw):
    return lax.dot_general(a, b, (((1,), (1,)), ((), ())), preferred_element_type=F32, **kw)


def _tn(a, b, **kw):
    return lax.dot_general(a, b, (((0,), (0,)), ((), ())), preferred_element_type=F32, **kw)


def _mm(a, b, **kw):
    return jnp.dot(a, b, preferred_element_type=F32, **kw)


def _params(*sem):
    return pltpu.CompilerParams(dimension_semantics=sem, vmem_limit_bytes=VMEM_LIMIT)


def _const_spec(shape, index_map):
    return pl.BlockSpec(shape, index_map, pipeline_mode=pl.Buffered(1))


def _mod_kernel(c_ref, w_ref, b_ref, o_ref):
    cond = c_ref[...]
    sc = cond * _sigmoid(cond)
    o_ref[0] = _mm(sc, w_ref[0], precision=HIGHEST) + b_ref[0]


def _modulation(cond, w_mod, b_mod):
    depth, d, n = w_mod.shape
    rows = cond.shape[0]
    tn = d
    return pl.pallas_call(
        _mod_kernel,
        grid=(depth, n // tn),
        in_specs=[pl.BlockSpec((rows, d), lambda l, j: (0, 0)),
                  pl.BlockSpec((1, d, tn), lambda l, j: (l, 0, j)),
                  pl.BlockSpec((1, 1, tn), lambda l, j: (l, 0, j))],
        out_specs=pl.BlockSpec((1, rows, tn), lambda l, j: (l, 0, j)),
        out_shape=jax.ShapeDtypeStruct((depth, rows, n), F32),
        compiler_params=_params("parallel", "parallel"),
        name="modulation",
    )(cond, w_mod, b_mod.reshape(depth, 1, n))


def _halo_specs(tm, d, seq):
    nh = seq // HALO
    per = tm // HALO
    main = pl.BlockSpec((1, tm, d), lambda b, i: (b, i, 0))
    prev = pl.BlockSpec((1, HALO, d), lambda b, i: (b, jnp.maximum(i * per - 1, 0), 0))
    nxt = pl.BlockSpec((1, HALO, d), lambda b, i: (b, jnp.minimum((i + 1) * per, nh - 1), 0))
    return main, prev, nxt


def _modulated_tile(xm_ref, xp_ref, xn_ref, gain, shift, scale, tm, seq):
    i = pl.program_id(1)
    mult = gain * (1.0 + scale)
    mod = lambda x: _rms(x) * mult + shift
    before = jnp.where(i > 0, mod(xp_ref[0]), 0.0)
    after = jnp.where((i + 1) * tm < seq, mod(xn_ref[0]), 0.0)
    return jnp.concatenate([before, mod(xm_ref[0]), after], axis=0)


def _conv3_taps(r_ref, w, tm):
    return (r_ref[HALO - 1:HALO - 1 + tm, :] * w[0:1] + r_ref[HALO:HALO + tm, :] * w[1:2]
            + r_ref[HALO + 1:HALO + 1 + tm, :] * w[2:3])


def _conv3(r_ref, w_ref, cols, tm):
    return _conv3_taps(r_ref, w_ref[0, :, cols], tm)


def _rope(r, cos, sin):
    lane = _iota((1, LANES), 1)
    first = (lane % 32) < 16
    swapped = jnp.where(first, pltpu.roll(r, LANES - 16, 1), pltpu.roll(r, 16, 1))
    return r * cos + swapped * sin


def _inproj_kernel(xm_ref, xp_ref, xn_ref, mod_ref, g_ref, w_ref, wt_ref, gb_ref, qkc_ref, scw_ref, pw_ref,
                   ps_ref, cos_ref, sin_ref,
                   qk_out, vt_out, ogt_out, gates_out, yb_out, aq_out, ak_out, avt_out, yd_out,
                   u_scr, um_scr, r_scr, *, tm, seq):
    i = pl.program_id(1)
    u = _modulated_tile(xm_ref, xp_ref, xn_ref, g_ref[0, 0:1, :], mod_ref[0, 0, 0:1, :], mod_ref[0, 0, 1:2, :],
                        tm, seq)
    u_scr[...] = u.astype(BF16)
    um_scr[...] = u[HALO:HALO + tm].astype(BF16)

    def wcols(name, width):
        o = _OFF[name]
        return w_ref[0, :, o:o + width]

    for c, name in enumerate(("q", "k")):
        cols = slice(c * BRANCH_W, (c + 1) * BRANCH_W)
        r_scr[...] = _mm(u_scr[...], wcols(name, BRANCH_W))
        half = _conv3_taps(r_scr, 0.5 * qkc_ref[0, :, cols], tm)
        a = half + half * jnp.tanh(half)
        if name == "k":
            a = a * MLSTM_HD ** -0.5
        qk_out[0, :, cols] = a.astype(BF16)

    um = um_scr[...]

    def trows(name, height):
        o = _TOFF[name]
        return wt_ref[0, o:o + height, :]

    vt_out[0] = _nt(trows("v", BRANCH_W), um).astype(BF16)
    ogt_out[0] = _sigmoid(_nt(trows("o", BRANCH_W), um)).astype(BF16)
    avt_out[0] = _nt(trows("av", LANES), um).astype(BF16)

    g = _nt(trows("gates", 16), um) + gb_ref[0]
    logsig = jnp.minimum(g, 0.0) - jnp.log(1.0 + jnp.exp(-jnp.abs(g)))
    is_forget = (_iota((16, 1), 0) % 8) >= MLSTM_HEADS
    g = jnp.where(is_forget, logsig, g) * LOG2E
    gates_out[0, 0] = g[0:8]
    gates_out[0, 1] = g[8:16]

    r_scr[...] = _mm(u_scr[...], wcols("pool", BRANCH_W))
    assert max(POOL_HALF) <= HALO
    t_top = i * tm + _iota((HALO, 1), 0)
    t_bot = t_top + (tm - HALO)
    for gi, half in enumerate(POOL_HALF):
        cols = slice(gi * POOL_GW, (gi + 1) * POOL_GW)
        acc = r_scr[HALO - half:HALO - half + tm, cols]
        for o in range(-half + 1, half):
            acc = acc + r_scr[HALO + o:HALO + o + tm, cols]
        inv = lambda t: 1.0 / (jnp.minimum(t + half, seq) - jnp.maximum(t - half, 0)).astype(F32)
        mean = jnp.concatenate([acc[:HALO] * inv(t_top), acc[HALO:tm - HALO] * (0.5 / half),
                                acc[tm - HALO:] * inv(t_bot)], axis=0)
        delta = mean - r_scr[HALO:HALO + tm, cols]
        y = _mm(delta.astype(BF16), pw_ref[0, gi]) * ps_ref[0, 0:1, cols]
        yb_out[0, :, cols] = y.astype(BF16)

    cos = cos_ref[...]
    sin = sin_ref[...]
    rq = _mm(um, wcols("aq", BRANCH_W))
    for j in range(BRANCH_W // LANES):
        cols = slice(j * LANES, (j + 1) * LANES)
        aq_out[0, :, cols] = (_rope(rq[:, cols], cos, sin) * ATT_Q_SCALE).astype(BF16)
    ak_out[0] = _rope(_mm(um, wcols("ak", LANES)), cos, sin).astype(BF16)

    rb = _mm(um, wcols("sb", BRANCH_W))
    r_scr[...] = _mm(u_scr[...], wcols("sc", BRANCH_W)) * _mm(u_scr[...], wcols("sx", BRANCH_W))
    yd_out[0] = (rb * _conv3(r_scr, scw_ref, slice(0, BRANCH_W), tm)).astype(BF16)


def _inproj(x, mod, norm_g, w1, wt, gb, qkc, scw, pw, ps, cos, sin, *, layer, mod_row, tm):
    bsz, seq, d = x.shape
    nt = seq // tm
    main, prev, nxt = _halo_specs(tm, d, seq)
    mrow = (lambda b: b) if mod_row is None else (lambda b: mod_row)
    lsel3 = lambda b, i: (layer, 0, 0)
    in_specs = [
        main, prev, nxt,
        pl.BlockSpec((1, 1, 6, d), lambda b, i: (layer, mrow(b), 0, 0)),
        _const_spec((1, 4, d), lsel3),
        _const_spec((1, d, W1_COLS), lsel3),
        _const_spec((1, WT_ROWS, d), lsel3),
        _const_spec((1, 16, 1), lsel3),
        _const_spec((1, 3, 2 * BRANCH_W), lsel3),
        _const_spec((1, 3, BRANCH_W), lsel3),
        _const_spec((1, len(POOL_HALF), POOL_GW, POOL_GW), lambda b, i: (layer, 0, 0, 0)),
        _const_spec((1, 1, BRANCH_W), lsel3),
        pl.BlockSpec((tm, LANES), lambda b, i: (i, 0)),
        pl.BlockSpec((tm, LANES), lambda b, i: (i, 0)),
    ]
    tok = lambda w: pl.BlockSpec((1, tm, w), lambda b, i: (b, i, 0))
    feat = lambda w: pl.BlockSpec((1, w, tm), lambda b, i: (b, 0, i))
    out_specs = [tok(2 * BRANCH_W), feat(BRANCH_W), feat(BRANCH_W),
                 pl.BlockSpec((1, 2, 8, tm), lambda b, i: (b, 0, 0, i)),
                 tok(BRANCH_W), tok(BRANCH_W), tok(LANES), feat(LANES), tok(BRANCH_W)]
    act = lambda w: jax.ShapeDtypeStruct((bsz, seq, w), BF16)
    act_t = lambda w: jax.ShapeDtypeStruct((bsz, w, seq), BF16)
    out_shape = [act(2 * BRANCH_W), act_t(BRANCH_W), act_t(BRANCH_W),
                 jax.ShapeDtypeStruct((bsz, 2, 8, seq), F32),
                 act(BRANCH_W), act(BRANCH_W), act(LANES), act_t(LANES), act(BRANCH_W)]
    return pl.pallas_call(
        functools.partial(_inproj_kernel, tm=tm, seq=seq),
        grid=(bsz, nt),
        in_specs=in_specs,
        out_specs=out_specs,
        out_shape=out_shape,
        scratch_shapes=[pltpu.VMEM((tm + 2 * HALO, d), BF16), pltpu.VMEM((tm, d), BF16),
                        pltpu.VMEM((tm + 2 * HALO, BRANCH_W), F32)],
        compiler_params=_params("parallel", "parallel"),
        name="inproj",
    )(x, x, x, mod, norm_g, w1, wt, gb, qkc, scw, pw, ps, cos, sin)


def _split3(x):
    x = jnp.concatenate([x, jnp.zeros_like(x)], axis=0)
    hi = x.astype(BF16)
    r1 = x - hi.astype(F32)
    mid = r1.astype(BF16)
    lo = (r1 - mid.astype(F32)).astype(BF16)
    return jnp.concatenate([hi, mid, lo], axis=0)


def _sum3(y):
    return (y[0:8] + y[16:24]) + y[32:40]


def _mlstm_kernel(qk_ref, vt_ref, g_ref, c0_ref, m0_ref, h_out, c_out, m_out, *, chunk, reverse):
    L = chunk
    hd = MLSTM_HD
    nck = qk_ref.shape[1] // L

    @pl.when(pl.program_id(1) == 0)
    def _():
        c_out[...] = c0_ref[...]
        m_out[...] = m0_ref[...]

    s_idx = _iota((L, L), 0)
    t_idx = _iota((L, L), 1)
    seen = (s_idx >= t_idx) if reverse else (s_idx <= t_idx)
    seen_bf = seen.astype(BF16)
    is_last = _iota((1, L), 1) == (0 if reverse else L - 1)
    ones = jnp.ones((hd, L), BF16)
    neg_inf = -jnp.inf

    heads = range(MLSTM_HEADS)
    wide = lambda parts: jnp.concatenate(parts, axis=1)
    toks = [slice(((nck - 1 - i) if reverse else i) * L, (((nck - 1 - i) if reverse else i) + 1) * L)
            for i in range(nck)]

    qs, ks, vaugs, b_rows, ig_rows, keys, scores = [], [], [], [], [], [], []
    for tok in toks:
        g = g_ref[0, 0, :, tok]
        g3 = _split3(g)
        cum_row = _sum3(_mm(g3, seen_bf))
        key_rows = cum_row[MLSTM_HEADS:] - g[:MLSTM_HEADS]
        key_cols = jnp.transpose(jnp.concatenate([key_rows, jnp.zeros_like(key_rows)], axis=0))
        for h in heads:
            qs.append(qk_ref[0, tok, h * hd:(h + 1) * hd])
            ks.append(qk_ref[0, tok, BRANCH_W + h * hd:BRANCH_W + (h + 1) * hd])
            vaugs.append(jnp.concatenate([vt_ref[0, h * hd:(h + 1) * hd, tok], ones], axis=0))
            b_rows.append(cum_row[MLSTM_HEADS + h:MLSTM_HEADS + h + 1, :])
            ig_rows.append(g[h:h + 1, :])
            keys.append(jnp.broadcast_to(key_cols[:, h:h + 1], (L, L)))
            scores.append(_nt(ks[-1], qs[-1]))
    n_pairs = len(qs)
    b_row = wide(b_rows)
    s_w = _iota((L, n_pairs * L), 0)
    t_w = _iota((L, n_pairs * L), 1) % L
    seen_w = (s_w >= t_w) if reverse else (s_w <= t_w)
    dmat = jnp.where(seen_w, b_row - wide(keys), neg_inf)
    m_intra = jnp.max(dmat, axis=0, keepdims=True)
    p = (wide(scores) * jnp.exp2(dmat - m_intra)).astype(BF16)
    b_tots = [jnp.max(jnp.where(is_last, b, neg_inf), axis=1, keepdims=True) for b in b_rows]
    u_loc = wide([bt - b + ig for bt, b, ig in zip(b_tots, b_rows, ig_rows)])
    m_locs = [jnp.max(u_loc[:, e * L:(e + 1) * L], axis=1, keepdims=True) for e in range(n_pairs)]
    wk_loc = jnp.exp2(u_loc - wide([jnp.broadcast_to(m, (1, L)) for m in m_locs]))
    r_intra = [_mm(vaugs[e], p[:, e * L:(e + 1) * L]) for e in range(n_pairs)]
    kv_loc = [_mm((vaugs[e].astype(F32) * wk_loc[:, e * L:(e + 1) * L]).astype(BF16), ks[e])
              for e in range(n_pairs)]

    for i, tok in enumerate(toks):
        for h in heads:
            e = i * MLSTM_HEADS + h
            lanes = slice(e * L, (e + 1) * L)
            m_prev = m_out[0, h:h + 1, 0:1]
            c_prev = c_out[0, h]
            a = b_rows[e] + m_prev
            mt = jnp.maximum(a, m_intra[:, lanes])
            r = (jnp.exp2(m_intra[:, lanes] - mt) * r_intra[e]
                 + jnp.exp2(a - mt) * _nt(c_prev.astype(BF16), qs[e]))
            den = jnp.maximum(jnp.abs(r[hd:]), jnp.exp2(-mt))
            h_out[0, h * hd:(h + 1) * hd, tok] = (r[:hd] / den).astype(BF16)
            m_new = jnp.max(jnp.where(is_last, mt, neg_inf), axis=1, keepdims=True)
            c_out[0, h] = (jnp.exp2(b_tots[e] + m_prev - m_new) * c_prev
                           + jnp.exp2(m_locs[e] - m_new) * kv_loc[e])
            m_out[0, h:h + 1, :] = jnp.broadcast_to(m_new, (1, LANES))


def _mlstm(qk, vt, gates, c0, m0, *, chunk, reverse):
    bsz, seq, _ = qk.shape
    nck = min(MLSTM_CHUNKS_PER_STEP, seq // chunk)
    tb = nck * chunk
    ns = seq // tb
    hd = MLSTM_HD
    direction = 1 if reverse else 0
    blk = (lambda j: ns - 1 - j) if reverse else (lambda j: j)
    state_c = pl.BlockSpec((1, MLSTM_HEADS, 2 * hd, hd), lambda b, j: (b, 0, 0, 0))
    state_m = pl.BlockSpec((1, 8, LANES), lambda b, j: (b, 0, 0))
    return pl.pallas_call(
        functools.partial(_mlstm_kernel, chunk=chunk, reverse=reverse),
        grid=(bsz, ns),
        in_specs=[pl.BlockSpec((1, tb, 2 * BRANCH_W), lambda b, j: (b, blk(j), 0)),
                  pl.BlockSpec((1, BRANCH_W, tb), lambda b, j: (b, 0, blk(j))),
                  pl.BlockSpec((1, 1, 8, tb), lambda b, j: (b, direction, 0, blk(j))),
                  state_c, state_m],
        out_specs=[pl.BlockSpec((1, BRANCH_W, tb), lambda b, j: (b, 0, blk(j))),
                   state_c, state_m],
        out_shape=[jax.ShapeDtypeStruct((bsz, BRANCH_W, seq), BF16),
                   jax.ShapeDtypeStruct((bsz, MLSTM_HEADS, 2 * hd, hd), F32),
                   jax.ShapeDtypeStruct((bsz, 8, LANES), F32)],
        compiler_params=_params("parallel", "arbitrary"),
        name="mlstm_bwd" if reverse else "mlstm_fwd",
    )(qk, vt, gates, c0, m0)


def _attn_kernel(*refs, nb, local):
    blk = ATT_BLOCK
    hd = ATT_HEAD_DIM
    if local:
        q_ref, kp_ref, kc_ref, kn_ref, kx_ref, vp_ref, vc_ref, vn_ref, vx_ref, sink_ref, o_ref = refs
        k_loc = jnp.concatenate([kp_ref[0], kc_ref[0], kn_ref[0]], axis=0)
        vt_loc = jnp.concatenate([vp_ref[0], vc_ref[0], vn_ref[0]], axis=1)
    else:
        q_ref, kx_ref, vx_ref, sink_ref, o_ref = refs
    nsub = q_ref.shape[1] // blk
    step = pl.program_id(1)
    lo = _iota((1, LANES), 1) < hd
    nq = ATT_Q_HEADS * blk
    sk = sink_ref[0] * LOG2E
    if local:
        kj = _iota((blk, nq), 0)
        qi = _iota((blk, nq), 1) % blk
        after = kj >= qi
        before = kj <= qi
        band_prev = jnp.where(after, 0.0, -jnp.inf)
        band_next = jnp.where(before, 0.0, -jnp.inf)

    scores, values, prev_bias, next_bias = [], [], [], []
    for i in range(nsub):
        if local:
            k = jnp.concatenate([k_loc[i * blk:(i + 3) * blk], kx_ref[0]], axis=0)
            vt = jnp.concatenate([vt_loc[:, i * blk:(i + 3) * blk], vx_ref[0]], axis=1)
            n = step * nsub + i
            prev_bias.append(band_prev if i > 0 else jnp.where(after & (n > 0), 0.0, -jnp.inf))
            next_bias.append(band_next if i < nsub - 1 else jnp.where(before & (n < nb - 1), 0.0, -jnp.inf))
        else:
            k = kx_ref[0]
            vt = vx_ref[0]
        nk = k.shape[0]
        values.append(jnp.concatenate([vt, jnp.ones((16, nk), BF16)], axis=0))
        parts = []
        for j in range(BRANCH_W // LANES):
            q2 = q_ref[0, i * blk:(i + 1) * blk, j * LANES:(j + 1) * LANES]
            zero = jnp.zeros_like(q2)
            parts += [jnp.where(lo, q2, zero), jnp.where(lo, zero, q2)]
        scores.append(_nt(k, jnp.concatenate(parts, axis=0)))
    s = jnp.concatenate(scores, axis=1)
    if local:
        s = jnp.concatenate([s[0:blk] + jnp.concatenate(prev_bias, axis=1), s[blk:2 * blk],
                             s[2 * blk:3 * blk] + jnp.concatenate(next_bias, axis=1), s[3 * blk:]], axis=0)
    sk = jnp.concatenate([sk] * nsub, axis=1)
    m = jnp.maximum(jnp.max(s, axis=0, keepdims=True), sk)
    e = jnp.exp2(s - m).astype(BF16)
    extra = jnp.exp2(sk - m)
    for i in range(nsub):
        r = _mm(values[i], e[:, i * nq:(i + 1) * nq])
        den = r[2 * hd:2 * hd + 1, :] + extra[:, i * nq:(i + 1) * nq]
        for p in range(ATT_Q_HEADS):
            c = p % 2
            head = p // 2 + 4 * c
            cols = slice(p * blk, (p + 1) * blk)
            o_ref[0, head * hd:(head + 1) * hd, i * blk:(i + 1) * blk] = (
                r[c * hd:(c + 1) * hd, cols] / den[:, cols]).astype(BF16)


def _attention(aq, ak, avt, ak_ctx, avt_ctx, sink, *, layer, local):
    bsz, seq, _ = aq.shape
    nb = seq // ATT_BLOCK
    nsub = min(ATT_SUB, nb)
    tq = nsub * ATT_BLOCK
    lc = ak_ctx.shape[1]
    qspec = pl.BlockSpec((1, tq, BRANCH_W), lambda b, n: (b, n, 0))
    kx_spec = pl.BlockSpec((1, lc, LANES), lambda b, n: (b, 0, 0))
    vx_spec = pl.BlockSpec((1, LANES, lc), lambda b, n: (b, 0, 0))
    sink_spec = pl.BlockSpec((1, 1, ATT_Q_HEADS * ATT_BLOCK), lambda b, n: (layer, 0, 0))
    if local:
        prev = lambda n: jnp.maximum(n * nsub - 1, 0)
        nxt = lambda n: jnp.minimum((n + 1) * nsub, nb - 1)
        kspecs = [pl.BlockSpec((1, ATT_BLOCK, LANES), lambda b, n: (b, prev(n), 0)),
                  pl.BlockSpec((1, tq, LANES), lambda b, n: (b, n, 0)),
                  pl.BlockSpec((1, ATT_BLOCK, LANES), lambda b, n: (b, nxt(n), 0))]
        vspecs = [pl.BlockSpec((1, LANES, ATT_BLOCK), lambda b, n: (b, 0, prev(n))),
                  pl.BlockSpec((1, LANES, tq), lambda b, n: (b, 0, n)),
                  pl.BlockSpec((1, LANES, ATT_BLOCK), lambda b, n: (b, 0, nxt(n)))]
        in_specs = [qspec, *kspecs, kx_spec, *vspecs, vx_spec, sink_spec]
        args = (aq, ak, ak, ak, ak_ctx, avt, avt, avt, avt_ctx, sink)
    else:
        in_specs = [qspec, kx_spec, vx_spec, sink_spec]
        args = (aq, ak_ctx, avt_ctx, sink)
    return pl.pallas_call(
        functools.partial(_attn_kernel, nb=nb, local=local),
        grid=(bsz, nb // nsub),
        in_specs=in_specs,
        out_specs=pl.BlockSpec((1, BRANCH_W, tq), lambda b, n: (b, 0, n)),
        out_shape=jax.ShapeDtypeStruct((bsz, BRANCH_W, seq), BF16),
        compiler_params=_params("parallel", "parallel"),
        name="attention",
    )(*args)


def _merge_kernel(x_ref, mod_ref, g_ref, hf_ref, hb_ref, og_ref, yb_ref, att_ref, yd_ref, wg_ref, wb_ref, wo_ref,
                  o_ref):
    d = D_MODEL
    tm = x_ref.shape[1]
    parts = MERGE_PARTS if tm % (MERGE_PARTS * 2 * LANES) == 0 else 1
    rows = tm // parts
    for part in range(parts):
        tok = slice(part * rows, (part + 1) * rows)
        x = x_ref[0, tok, :]
        u = (_rms(x) * g_ref[0, 0:1, :] * (1.0 + mod_ref[0, 0, 1:2, :]) + mod_ref[0, 0, 0:1, :]).astype(BF16)
        ya_t = (og_ref[0, :, tok].astype(F32)
                * (hf_ref[0, :, tok].astype(F32) + hb_ref[0, :, tok].astype(F32))).astype(BF16)
        ys = (ya_t, yb_ref[0, tok, :], att_ref[0, :, tok], yd_ref[0, tok, :])
        feature_major = (True, False, True, False)
        merged = None
        for n in range(N_BRANCH):
            gate = _sigmoid(_mm(u, wg_ref[0, :, n * d:(n + 1) * d]))
            proj = _tn(ys[n], wb_ref[0, n]) if feature_major[n] else _mm(ys[n], wb_ref[0, n])
            term = gate * proj
            merged = term if merged is None else merged + term
        out = _mm(merged.astype(BF16), wo_ref[0])
        o_ref[0, tok, :] = x + mod_ref[0, 0, 2:3, :] * (_rms(out) * g_ref[0, 1:2, :])


def _merge(x, mod, norm_g, hf, hb, og, yb, att, yd, w_mg, w_branch, w_out, *, layer, mod_row, tm):
    bsz, seq, d = x.shape
    mrow = (lambda b: b) if mod_row is None else (lambda b: mod_row)
    tok = lambda w: pl.BlockSpec((1, tm, w), lambda b, i: (b, i, 0))
    feat = pl.BlockSpec((1, BRANCH_W, tm), lambda b, i: (b, 0, i))
    lsel3 = lambda b, i: (layer, 0, 0)
    return pl.pallas_call(
        _merge_kernel,
        grid=(bsz, seq // tm),
        in_specs=[tok(d),
                  pl.BlockSpec((1, 1, 6, d), lambda b, i: (layer, mrow(b), 0, 0)),
                  _const_spec((1, 4, d), lsel3),
                  feat, feat, feat, tok(BRANCH_W), feat, tok(BRANCH_W),
                  _const_spec((1, d, N_BRANCH * d), lsel3),
                  _const_spec((1, N_BRANCH, BRANCH_W, d), lambda b, i: (layer, 0, 0, 0)),
                  _const_spec((1, d, d), lsel3)],
        out_specs=tok(d),
        out_shape=jax.ShapeDtypeStruct((bsz, seq, d), F32),
        compiler_params=_params("parallel", "parallel"),
        name="merge",
    )(x, mod, norm_g, hf, hb, og, yb, att, yd, w_mg, w_branch, w_out)


def _ffn_kernel(xm_ref, xp_ref, xn_ref, mod_ref, g_ref, wu_ref, wc_ref, wd_ref, o_ref, u_scr, rg_scr, rv_scr,
                h_scr, *, tm, seq):
    u = _modulated_tile(xm_ref, xp_ref, xn_ref, g_ref[0, 2:3, :], mod_ref[0, 0, 3:4, :], mod_ref[0, 0, 4:5, :],
                        tm, seq)
    u_scr[...] = u.astype(BF16)
    cw = FF_CHUNK
    for c in range(D_FF // cw):
        gcols = slice(c * cw, (c + 1) * cw)
        vcols = slice(D_FF + c * cw, D_FF + (c + 1) * cw)
        rg_scr[...] = _mm(u_scr[...], wu_ref[0, :, gcols])
        rv_scr[...] = _mm(u_scr[...], wu_ref[0, :, vcols])
        half = _conv3_taps(rg_scr, 0.5 * wc_ref[0, :, gcols], tm)
        val = _conv3(rv_scr, wc_ref, vcols, tm)
        h_scr[:, gcols] = ((half + half * jnp.tanh(half)) * val).astype(BF16)
    out = _mm(h_scr[...], wd_ref[0])
    x = xm_ref[0]
    o_ref[0] = x + mod_ref[0, 0, 5:6, :] * (_rms(out) * g_ref[0, 3:4, :])


def _ffn(x, mod, norm_g, w_up, w_conv, w_down, *, layer, mod_row, tm):
    bsz, seq, d = x.shape
    main, prev, nxt = _halo_specs(tm, d, seq)
    mrow = (lambda b: b) if mod_row is None else (lambda b: mod_row)
    lsel3 = lambda b, i: (layer, 0, 0)
    return pl.pallas_call(
        functools.partial(_ffn_kernel, tm=tm, seq=seq),
        grid=(bsz, seq // tm),
        in_specs=[main, prev, nxt,
                  pl.BlockSpec((1, 1, 6, d), lambda b, i: (layer, mrow(b), 0, 0)),
                  _const_spec((1, 4, d), lsel3),
                  _const_spec((1, d, 2 * D_FF), lsel3),
                  _const_spec((1, 3, 2 * D_FF), lsel3),
                  _const_spec((1, D_FF, d), lsel3)],
        out_specs=main,
        out_shape=jax.ShapeDtypeStruct((bsz, seq, d), F32),
        scratch_shapes=[pltpu.VMEM((tm + 2 * HALO, d), BF16), pltpu.VMEM((tm + 2 * HALO, FF_CHUNK), F32),
                        pltpu.VMEM((tm + 2 * HALO, FF_CHUNK), F32), pltpu.VMEM((tm, D_FF), BF16)],
        compiler_params=_params("parallel", "parallel"),
        name="convffn",
    )(x, x, x, mod, norm_g, w_up, w_conv, w_down)


def _rope_tables(seq):
    t = jnp.arange(seq, dtype=jnp.int32)[:, None]
    lane = jnp.arange(LANES, dtype=jnp.int32)[None, :]
    in_head = lane % ATT_HEAD_DIM
    n_freq = ATT_HEAD_DIM // 4
    inv_freq = ROPE_BASE ** (-(in_head % n_freq).astype(F32) / n_freq)
    pos = jnp.where(in_head < ATT_HEAD_DIM // 2, t // GRID_W, t % GRID_W).astype(F32)
    ang = pos * inv_freq
    sign = jnp.where((in_head % (2 * n_freq)) < n_freq, -1.0, 1.0)
    return jnp.cos(ang), sign * jnp.sin(ang)


def _tile_rows(seq):
    return min(seq, 512)


def kernel(x, c, ctx, c_ctx, w_mod, b_mod, norm_g, w_in, mlstm_qk_conv, mlstm_gate_bias, pool_w, pool_scale,
           attn_sink, sconv_w, w_branch, w_out, ffn_up, ffn_conv, ffn_down):
    bsz, seq, d = x.shape
    lc = ctx.shape[1]
    depth = w_mod.shape[0]
    assert d == D_MODEL and seq % 512 == 0 and lc % 256 == 0 and bsz + 1 <= 8

    cond = jnp.concatenate([c, c_ctx[None, :], jnp.zeros((8 - bsz - 1, d), F32)], axis=0)
    mod = _modulation(cond, w_mod, b_mod).reshape(depth, 8, 6, d)

    g0 = BRANCH_W * 4
    g1 = g0 + 4 * MLSTM_HEADS
    mg0 = w_in.shape[-1] - N_BRANCH * d
    p0 = g1
    aq0 = p0 + BRANCH_W
    ak0 = aq0 + BRANCH_W
    av0 = ak0 + ATT_KV_HEADS * ATT_HEAD_DIM
    sc0 = av0 + ATT_KV_HEADS * ATT_HEAD_DIM
    w_aq = w_in[:, :, aq0:ak0].reshape(depth, d, ATT_KV_HEADS, ATT_Q_HEADS // ATT_KV_HEADS, ATT_HEAD_DIM)
    w_aq = jnp.swapaxes(w_aq, 2, 3).reshape(depth, d, BRANCH_W)
    v0 = 2 * BRANCH_W
    w1 = jnp.concatenate([w_in[:, :, :v0], w_in[:, :, p0:aq0], w_aq, w_in[:, :, ak0:av0], w_in[:, :, sc0:mg0]],
                         axis=-1).astype(BF16)
    perm = jnp.array(_GATE_PERM)
    wt = jnp.concatenate([w_in[:, :, v0:g0], w_in[:, :, av0:sc0], w_in[:, :, g0:g1][:, :, perm]], axis=-1)
    wt = jnp.swapaxes(wt, 1, 2).astype(BF16)
    gb = mlstm_gate_bias.reshape(depth, 4 * MLSTM_HEADS)[:, perm][:, :, None]
    w_mg = w_in[:, :, mg0:].astype(BF16)
    pw = pool_w.astype(BF16)
    ps = pool_scale[:, None, :]
    sink_heads = attn_sink.reshape(depth, ATT_KV_HEADS, ATT_Q_HEADS // ATT_KV_HEADS).swapaxes(1, 2)
    sink = jnp.broadcast_to(sink_heads.reshape(depth, 1, ATT_Q_HEADS, 1),
                            (depth, 1, ATT_Q_HEADS, ATT_BLOCK)).reshape(depth, 1, ATT_Q_HEADS * ATT_BLOCK)
    wb = w_branch.astype(BF16)
    wo = w_out.astype(BF16)
    wu = ffn_up.astype(BF16)
    wc = ffn_conv
    wd = ffn_down.astype(BF16)

    cos_x, sin_x = _rope_tables(seq)
    cos_c, sin_c = jnp.ones((lc, LANES), F32), jnp.zeros((lc, LANES), F32)

    hd = MLSTM_HD
    zero_c = jnp.zeros((bsz, MLSTM_HEADS, 2 * hd, hd), F32)
    zero_m = jnp.zeros((bsz, 8, LANES), F32)
    tm_x, tm_c = _tile_rows(seq), _tile_rows(lc)
    scan = functools.partial(_mlstm, chunk=MLSTM_CHUNK)

    h = ctx
    for l in range(depth):
        ctx_out = l < depth - 1
        proj = functools.partial(_inproj, mod=mod, norm_g=norm_g, w1=w1, wt=wt, gb=gb, qkc=mlstm_qk_conv,
                                 scw=sconv_w, pw=pw, ps=ps, layer=l)
        qk_c, v_c, og_c, gt_c, yb_c, aq_c, ak_c, avt_c, yd_c = proj(h, cos=cos_c, sin=sin_c, mod_row=bsz, tm=tm_c)
        qk_x, v_x, og_x, gt_x, yb_x, aq_x, ak_x, avt_x, yd_x = proj(x, cos=cos_x, sin=sin_x, mod_row=None, tm=tm_x)
        hf_c, cf, mf = scan(qk_c, v_c, gt_c, zero_c, zero_m, reverse=False)
        hb_c, cb, mb = scan(qk_c, v_c, gt_c, zero_c, zero_m, reverse=True)
        hf_x, _, _ = scan(qk_x, v_x, gt_x, cf, mf, reverse=False)
        hb_x, _, _ = scan(qk_x, v_x, gt_x, cb, mb, reverse=True)
        att_x = _attention(aq_x, ak_x, avt_x, ak_c, avt_c, sink, layer=l, local=True)
        tail = dict(norm_g=norm_g, layer=l)
        x = _merge(x, mod, hf=hf_x, hb=hb_x, og=og_x, yb=yb_x, att=att_x, yd=yd_x, w_mg=w_mg, w_branch=wb,
                   w_out=wo, mod_row=None, tm=tm_x, **tail)
        x = _ffn(x, mod, w_up=wu, w_conv=wc, w_down=wd, mod_row=None, tm=tm_x, **tail)
        if ctx_out:
            att_c = _attention(aq_c, ak_c, avt_c, ak_c, avt_c, sink, layer=l, local=False)
            h = _merge(h, mod, hf=hf_c, hb=hb_c, og=og_c, yb=yb_c, att=att_c, yd=yd_c, w_mg=w_mg, w_branch=wb,
                       w_out=wo, mod_row=bsz, tm=tm_c, **tail)
            h = _ffn(h, mod, w_up=wu, w_conv=wc, w_down=wd, mod_row=bsz, tm=tm_c, **tail)
    return x
```

```python
import functools

import jax
import jax.numpy as jnp
from jax import lax
from jax.experimental import pallas as pl
from jax.experimental.pallas import tpu as pltpu

F32 = jnp.float32
BF16 = jnp.bfloat16
HIGHEST = lax.Precision.HIGHEST

D_MODEL = 1024
GRID_W = 64
N_BRANCH = 4
BRANCH_W = D_MODEL // 2
MLSTM_HEADS = 4
MLSTM_HD = BRANCH_W // MLSTM_HEADS
MLSTM_CHUNK = 256
MLSTM_CHUNKS_PER_STEP = 8
POOL_HALF = (1, 2, 4, 8)
POOL_GW = BRANCH_W // len(POOL_HALF)
ATT_HEAD_DIM = 64
ATT_Q_HEADS = BRANCH_W // ATT_HEAD_DIM
ATT_KV_HEADS = 2
ATT_BLOCK = 128
ATT_SUB = 8
LOG2E = 1.4426950408889634
ATT_Q_SCALE = ATT_HEAD_DIM ** -0.5 * LOG2E
ROPE_BASE = 10000.0
D_FF = 2816
EPS = 1e-6

LANES = 128
HALO = 8
FF_CHUNK = 256
MERGE_PARTS = 2
VMEM_LIMIT = 56 * 1024 * 1024

_OFF = {}
_acc = 0
for _name, _w in (("q", 512), ("k", 512), ("pool", 512), ("aq", 512), ("ak", 128), ("sb", 512), ("sc", 512),
                  ("sx", 512)):
    _OFF[_name] = _acc
    _acc += _w
W1_COLS = _acc
_TOFF = {}
_acc = 0
for _name, _w in (("v", 512), ("o", 512), ("av", 128), ("gates", 16)):
    _TOFF[_name] = _acc
    _acc += _w
WT_ROWS = _acc
_GATE_PERM = (0, 1, 2, 3, 8, 9, 10, 11, 4, 5, 6, 7, 12, 13, 14, 15)


def _iota(shape, dim):
    return lax.broadcasted_iota(jnp.int32, shape, dim)


def _rms(x):
    return x * lax.rsqrt(jnp.mean(x * x, axis=-1, keepdims=True) + EPS)


def _sigmoid(x):
    return 1.0 / (1.0 + jnp.exp(-x))


def _nt(a, b, **kw):
    return lax.dot_general(a, b, (((1,), (1,)), ((), ())), preferred_element_type=F32, **kw)


def _tn(a, b, **kw):
    return lax.dot_general(a, b, (((0,), (0,)), ((), ())), preferred_element_type=F32, **kw)


def _mm(a, b, **kw):
    return jnp.dot(a, b, preferred_element_type=F32, **kw)


def _params(*sem):
    return pltpu.CompilerParams(dimension_semantics=sem, vmem_limit_bytes=VMEM_LIMIT)


def _const_spec(shape, index_map):
    return pl.BlockSpec(shape, index_map, pipeline_mode=pl.Buffered(1))


def _mod_kernel(c_ref, w_ref, b_ref, o_ref):
    cond = c_ref[...]
    sc = cond * _sigmoid(cond)
    o_ref[0] = _mm(sc, w_ref[0], precision=HIGHEST) + b_ref[0]


def _modulation(cond, w_mod, b_mod):
    depth, d, n = w_mod.shape
    rows = cond.shape[0]
    tn = d
    return pl.pallas_call(
        _mod_kernel,
        grid=(depth, n // tn),
        in_specs=[pl.BlockSpec((rows, d), lambda l, j: (0, 0)),
                  pl.BlockSpec((1, d, tn), lambda l, j: (l, 0, j)),
                  pl.BlockSpec((1, 1, tn), lambda l, j: (l, 0, j))],
        out_specs=pl.BlockSpec((1, rows, tn), lambda l, j: (l, 0, j)),
        out_shape=jax.ShapeDtypeStruct((depth, rows, n), F32),
        compiler_params=_params("parallel", "parallel"),
        name="modulation",
    )(cond, w_mod, b_mod.reshape(depth, 1, n))


def _halo_specs(tm, d, seq):
    nh = seq // HALO
    per = tm // HALO
    main = pl.BlockSpec((1, tm, d), lambda b, i: (b, i, 0))
    prev = pl.BlockSpec((1, HALO, d), lambda b, i: (b, jnp.maximum(i * per - 1, 0), 0))
    nxt = pl.BlockSpec((1, HALO, d), lambda b, i: (b, jnp.minimum((i + 1) * per, nh - 1), 0))
    return main, prev, nxt


def _modulated_tile(xm_ref, xp_ref, xn_ref, gain, shift, scale, tm, seq):
    i = pl.program_id(1)
    mult = gain * (1.0 + scale)
    mod = lambda x: _rms(x) * mult + shift
    before = jnp.where(i > 0, mod(xp_ref[0]), 0.0)
    after = jnp.where((i + 1) * tm < seq, mod(xn_ref[0]), 0.0)
    return jnp.concatenate([before, mod(xm_ref[0]), after], axis=0)


def _conv3_taps(r_ref, w, tm):
    return (r_ref[HALO - 1:HALO - 1 + tm, :] * w[0:1] + r_ref[HALO:HALO + tm, :] * w[1:2]
            + r_ref[HALO + 1:HALO + 1 + tm, :] * w[2:3])


def _conv3(r_ref, w_ref, cols, tm):
    return _conv3_taps(r_ref, w_ref[0, :, cols], tm)


def _rope(r, cos, sin):
    lane = _iota((1, LANES), 1)
    first = (lane % 32) < 16
    swapped = jnp.where(first, pltpu.roll(r, LANES - 16, 1), pltpu.roll(r, 16, 1))
    return r * cos + swapped * sin


def _inproj_kernel(xm_ref, xp_ref, xn_ref, mod_ref, g_ref, w_ref, wt_ref, gb_ref, qkc_ref, scw_ref, pw_ref,
                   ps_ref, cos_ref, sin_ref,
                   qk_out, vt_out, ogt_out, gates_out, yb_out, aq_out, ak_out, avt_out, yd_out,
                   u_scr, um_scr, r_scr, r2_scr, *, tm, seq):
    i = pl.program_id(1)
    u = _modulated_tile(xm_ref, xp_ref, xn_ref, g_ref[0, 0:1, :], mod_ref[0, 0, 0:1, :], mod_ref[0, 0, 1:2, :],
                        tm, seq)
    u_scr[...] = u.astype(BF16)
    um_scr[...] = u[HALO:HALO + tm].astype(BF16)

    def wcols(name, width):
        o = _OFF[name]
        return w_ref[0, :, o:o + width]

    for c, name in enumerate(("q", "k")):
        cols = slice(c * BRANCH_W, (c + 1) * BRANCH_W)
        r_scr[...] = _mm(u_scr[...], wcols(name, BRANCH_W))
        half = _conv3_taps(r_scr, 0.5 * qkc_ref[0, :, cols], tm)
        a = half + half * jnp.tanh(half)
        if name == "k":
            a = a * MLSTM_HD ** -0.5
        qk_out[0, :, cols] = a.astype(BF16)

    um = um_scr[...]

    def trows(name, height):
        o = _TOFF[name]
        return wt_ref[0, o:o + height, :]

    g = _nt(trows("gates", 16), um) + gb_ref[0]
    logsig = jnp.minimum(g, 0.0) - jnp.log(1.0 + jnp.exp(-jnp.abs(g)))
    is_forget = (_iota((16, 1), 0) % 8) >= MLSTM_HEADS
    g = jnp.where(is_forget, logsig, g) * LOG2E
    gates_out[0, 0] = g[0:8]
    gates_out[0, 1] = g[8:16]

    r_scr[...] = _mm(u_scr[...], wcols("pool", BRANCH_W))
    assert max(POOL_HALF) <= HALO
    t_top = i * tm + _iota((HALO, 1), 0)
    t_bot = t_top + (tm - HALO)
    for gi, half in enumerate(POOL_HALF):
        cols = slice(gi * POOL_GW, (gi + 1) * POOL_GW)
        acc = r_scr[HALO - half:HALO - half + tm, cols]
        for o in range(-half + 1, half):
            acc = acc + r_scr[HALO + o:HALO + o + tm, cols]
        inv = lambda t: 1.0 / (jnp.minimum(t + half, seq) - jnp.maximum(t - half, 0)).astype(F32)
        mean = jnp.concatenate([acc[:HALO] * inv(t_top), acc[HALO:tm - HALO] * (0.5 / half),
                                acc[tm - HALO:] * inv(t_bot)], axis=0)
        delta = mean - r_scr[HALO:HALO + tm, cols]
        y = _mm(delta.astype(BF16), pw_ref[0, gi]) * ps_ref[0, 0:1, cols]
        yb_out[0, :, cols] = y.astype(BF16)
        if gi == 0:
            rb = _mm(um, wcols("sb", BRANCH_W))
        elif gi == 1:
            rc = _mm(u_scr[...], wcols("sc", BRANCH_W))
        elif gi == 2:
            r2_scr[...] = rc * _mm(u_scr[...], wcols("sx", BRANCH_W))
        else:
            rq = _mm(um, wcols("aq", BRANCH_W))

    cos = cos_ref[...]
    sin = sin_ref[...]
    for j in range(BRANCH_W // LANES):
        cols = slice(j * LANES, (j + 1) * LANES)
        aq_out[0, :, cols] = (_rope(rq[:, cols], cos, sin) * ATT_Q_SCALE).astype(BF16)
        if j == 0:
            vt_out[0] = _nt(trows("v", BRANCH_W), um).astype(BF16)
        elif j == 2:
            ogt_out[0] = _sigmoid(_nt(trows("o", BRANCH_W), um)).astype(BF16)
    avt_out[0] = _nt(trows("av", LANES), um).astype(BF16)
    ak_out[0] = _rope(_mm(um, wcols("ak", LANES)), cos, sin).astype(BF16)

    yd_out[0] = (rb * _conv3(r2_scr, scw_ref, slice(0, BRANCH_W), tm)).astype(BF16)


def _inproj(x, mod, norm_g, w1, wt, gb, qkc, scw, pw, ps, cos, sin, *, layer, mod_row, tm):
    bsz, seq, d = x.shape
    nt = seq // tm
    main, prev, nxt = _halo_specs(tm, d, seq)
    mrow = (lambda b: b) if mod_row is None else (lambda b: mod_row)
    lsel3 = lambda b, i: (layer, 0, 0)
    in_specs = [
        main, prev, nxt,
        pl.BlockSpec((1, 1, 6, d), lambda b, i: (layer, mrow(b), 0, 0)),
        _const_spec((1, 4, d), lsel3),
        _const_spec((1, d, W1_COLS), lsel3),
        _const_spec((1, WT_ROWS, d), lsel3),
        _const_spec((1, 16, 1), lsel3),
        _const_spec((1, 3, 2 * BRANCH_W), lsel3),
        _const_spec((1, 3, BRANCH_W), lsel3),
        _const_spec((1, len(POOL_HALF), POOL_GW, POOL_GW), lambda b, i: (layer, 0, 0, 0)),
        _const_spec((1, 1, BRANCH_W), lsel3),
        pl.BlockSpec((tm, LANES), lambda b, i: (i, 0)),
        pl.BlockSpec((tm, LANES), lambda b, i: (i, 0)),
    ]
    tok = lambda w: pl.BlockSpec((1, tm, w), lambda b, i: (b, i, 0))
    feat = lambda w: pl.BlockSpec((1, w, tm), lambda b, i: (b, 0, i))
    out_specs = [tok(2 * BRANCH_W), feat(BRANCH_W), feat(BRANCH_W),
                 pl.BlockSpec((1, 2, 8, tm), lambda b, i: (b, 0, 0, i)),
                 tok(BRANCH_W), tok(BRANCH_W), tok(LANES), feat(LANES), tok(BRANCH_W)]
    act = lambda w: jax.ShapeDtypeStruct((bsz, seq, w), BF16)
    act_t = lambda w: jax.ShapeDtypeStruct((bsz, w, seq), BF16)
    out_shape = [act(2 * BRANCH_W), act_t(BRANCH_W), act_t(BRANCH_W),
                 jax.ShapeDtypeStruct((bsz, 2, 8, seq), F32),
                 act(BRANCH_W), act(BRANCH_W), act(LANES), act_t(LANES), act(BRANCH_W)]
    return pl.pallas_call(
        functools.partial(_inproj_kernel, tm=tm, seq=seq),
        grid=(bsz, nt),
        in_specs=in_specs,
        out_specs=out_specs,
        out_shape=out_shape,
        scratch_shapes=[pltpu.VMEM((tm + 2 * HALO, d), BF16), pltpu.VMEM((tm, d), BF16),
                        pltpu.VMEM((tm + 2 * HALO, BRANCH_W), F32), pltpu.VMEM((tm + 2 * HALO, BRANCH_W), F32)],
        compiler_params=_params("parallel", "parallel"),
        name="inproj",
    )(x, x, x, mod, norm_g, w1, wt, gb, qkc, scw, pw, ps, cos, sin)


def _split3(x):
    x = jnp.concatenate([x, jnp.zeros_like(x)], axis=0)
    hi = x.astype(BF16)
    r1 = x - hi.astype(F32)
    mid = r1.astype(BF16)
    lo = (r1 - mid.astype(F32)).astype(BF16)
    return jnp.concatenate([hi, mid, lo], axis=0)


def _sum3(y):
    return (y[0:8] + y[16:24]) + y[32:40]


def _mlstm_kernel(qk_ref, vt_ref, g_ref, c0_ref, m0_ref, h_out, c_out, m_out, *, chunk, reverse):
    L = chunk
    hd = MLSTM_HD
    nck = qk_ref.shape[1] // L

    @pl.when(pl.program_id(1) == 0)
    def _():
        c_out[...] = c0_ref[...]
        m_out[...] = m0_ref[...]

    s_idx = _iota((L, L), 0)
    t_idx = _iota((L, L), 1)
    seen = (s_idx >= t_idx) if reverse else (s_idx <= t_idx)
    seen_bf = seen.astype(BF16)
    is_last = _iota((1, L), 1) == (0 if reverse else L - 1)
    ones = jnp.ones((hd, L), BF16)
    neg_inf = -jnp.inf

    heads = range(MLSTM_HEADS)
    wide = lambda parts: jnp.concatenate(parts, axis=1)
    toks = [slice(((nck - 1 - i) if reverse else i) * L, (((nck - 1 - i) if reverse else i) + 1) * L)
            for i in range(nck)]

    qs, ks, vaugs, b_rows, ig_rows, keys, scores = [], [], [], [], [], [], []
    for tok in toks:
        g = g_ref[0, 0, :, tok]
        g3 = _split3(g)
        cum_row = _sum3(_mm(g3, seen_bf))
        key_rows = cum_row[MLSTM_HEADS:] - g[:MLSTM_HEADS]
        key_cols = jnp.transpose(jnp.concatenate([key_rows, jnp.zeros_like(key_rows)], axis=0))
        for h in heads:
            qs.append(qk_ref[0, tok, h * hd:(h + 1) * hd])
            ks.append(qk_ref[0, tok, BRANCH_W + h * hd:BRANCH_W + (h + 1) * hd])
            vaugs.append(jnp.concatenate([vt_ref[0, h * hd:(h + 1) * hd, tok], ones], axis=0))
            b_rows.append(cum_row[MLSTM_HEADS + h:MLSTM_HEADS + h + 1, :])
            ig_rows.append(g[h:h + 1, :])
            keys.append(jnp.broadcast_to(key_cols[:, h:h + 1], (L, L)))
            scores.append(_nt(ks[-1], qs[-1]))
    n_pairs = len(qs)
    b_row = wide(b_rows)
    s_w = _iota((L, n_pairs * L), 0)
    t_w = _iota((L, n_pairs * L), 1) % L
    seen_w = (s_w >= t_w) if reverse else (s_w <= t_w)
    dmat = jnp.where(seen_w, b_row - wide(keys), neg_inf)
    m_intra = jnp.max(dmat, axis=0, keepdims=True)
    p = (wide(scores) * jnp.exp2(dmat - m_intra)).astype(BF16)
    b_tots = [jnp.max(jnp.where(is_last, b, neg_inf), axis=1, keepdims=True) for b in b_rows]
    u_loc = wide([bt - b + ig for bt, b, ig in zip(b_tots, b_rows, ig_rows)])
    m_locs = [jnp.max(u_loc[:, e * L:(e + 1) * L], axis=1, keepdims=True) for e in range(n_pairs)]
    wk_loc = jnp.exp2(u_loc - wide([jnp.broadcast_to(m, (1, L)) for m in m_locs]))
    r_intra = [_mm(vaugs[e], p[:, e * L:(e + 1) * L]) for e in range(n_pairs)]
    kv_loc = [_mm((vaugs[e].astype(F32) * wk_loc[:, e * L:(e + 1) * L]).astype(BF16), ks[e])
              for e in range(n_pairs)]

    for i, tok in enumerate(toks):
        for h in heads:
            e = i * MLSTM_HEADS + h
            lanes = slice(e * L, (e + 1) * L)
            m_prev = m_out[0, h:h + 1, 0:1]
            c_prev = c_out[0, h]
            a = b_rows[e] + m_prev
            mt = jnp.maximum(a, m_intra[:, lanes])
            r = (jnp.exp2(m_intra[:, lanes] - mt) * r_intra[e]
                 + jnp.exp2(a - mt) * _nt(c_prev.astype(BF16), qs[e]))
            den = jnp.maximum(jnp.abs(r[hd:]), jnp.exp2(-mt))
            h_out[0, h * hd:(h + 1) * hd, tok] = (r[:hd] / den).astype(BF16)
            m_new = jnp.max(jnp.where(is_last, mt, neg_inf), axis=1, keepdims=True)
            c_out[0, h] = (jnp.exp2(b_tots[e] + m_prev - m_new) * c_prev
                           + jnp.exp2(m_locs[e] - m_new) * kv_loc[e])
            m_out[0, h:h + 1, :] = jnp.broadcast_to(m_new, (1, LANES))


def _mlstm(qk, vt, gates, c0, m0, *, chunk, reverse):
    bsz, seq, _ = qk.shape
    nck = min(MLSTM_CHUNKS_PER_STEP, seq // chunk)
    tb = nck * chunk
    ns = seq // tb
    hd = MLSTM_HD
    direction = 1 if reverse else 0
    blk = (lambda j: ns - 1 - j) if reverse else (lambda j: j)
    state_c = pl.BlockSpec((1, MLSTM_HEADS, 2 * hd, hd), lambda b, j: (b, 0, 0, 0))
    state_m = pl.BlockSpec((1, 8, LANES), lambda b, j: (b, 0, 0))
    return pl.pallas_call(
        functools.partial(_mlstm_kernel, chunk=chunk, reverse=reverse),
        grid=(bsz, ns),
        in_specs=[pl.BlockSpec((1, tb, 2 * BRANCH_W), lambda b, j: (b, blk(j), 0)),
                  pl.BlockSpec((1, BRANCH_W, tb), lambda b, j: (b, 0, blk(j))),
                  pl.BlockSpec((1, 1, 8, tb), lambda b, j: (b, direction, 0, blk(j))),
                  state_c, state_m],
        out_specs=[pl.BlockSpec((1, BRANCH_W, tb), lambda b, j: (b, 0, blk(j))),
                   state_c, state_m],
        out_shape=[jax.ShapeDtypeStruct((bsz, BRANCH_W, seq), BF16),
                   jax.ShapeDtypeStruct((bsz, MLSTM_HEADS, 2 * hd, hd), F32),
                   jax.ShapeDtypeStruct((bsz, 8, LANES), F32)],
        compiler_params=_params("parallel", "arbitrary"),
        name="mlstm_bwd" if reverse else "mlstm_fwd",
    )(qk, vt, gates, c0, m0)


def _attn_kernel(*refs, nb, local):
    blk = ATT_BLOCK
    hd = ATT_HEAD_DIM
    if local:
        q_ref, kp_ref, kc_ref, kn_ref, kx_ref, vp_ref, vc_ref, vn_ref, vx_ref, sink_ref, o_ref = refs
        k_loc = jnp.concatenate([kp_ref[0], kc_ref[0], kn_ref[0]], axis=0)
        vt_loc = jnp.concatenate([vp_ref[0], vc_ref[0], vn_ref[0]], axis=1)
    else:
        q_ref, kx_ref, vx_ref, sink_ref, o_ref = refs
    nsub = q_ref.shape[1] // blk
    step = pl.program_id(1)
    lo = _iota((1, LANES), 1) < hd
    nq = ATT_Q_HEADS * blk
    sk = sink_ref[0] * LOG2E
    if local:
        kj = _iota((blk, nq), 0)
        qi = _iota((blk, nq), 1) % blk
        after = kj >= qi
        before = kj <= qi
        band_prev = jnp.where(after, 0.0, -jnp.inf)
        band_next = jnp.where(before, 0.0, -jnp.inf)

    scores, values, prev_bias, next_bias = [], [], [], []
    for i in range(nsub):
        if local:
            k = jnp.concatenate([k_loc[i * blk:(i + 3) * blk], kx_ref[0]], axis=0)
            vt = jnp.concatenate([vt_loc[:, i * blk:(i + 3) * blk], vx_ref[0]], axis=1)
            n = step * nsub + i
            prev_bias.append(band_prev if i > 0 else jnp.where(after & (n > 0), 0.0, -jnp.inf))
            next_bias.append(band_next if i < nsub - 1 else jnp.where(before & (n < nb - 1), 0.0, -jnp.inf))
        else:
            k = kx_ref[0]
            vt = vx_ref[0]
        nk = k.shape[0]
        values.append(jnp.concatenate([vt, jnp.ones((16, nk), BF16)], axis=0))
        parts = []
        for j in range(BRANCH_W // LANES):
            q2 = q_ref[0, i * blk:(i + 1) * blk, j * LANES:(j + 1) * LANES]
            zero = jnp.zeros_like(q2)
            parts += [jnp.where(lo, q2, zero), jnp.where(lo, zero, q2)]
        scores.append(_nt(k, jnp.concatenate(parts, axis=0)))
    s = jnp.concatenate(scores, axis=1)
    if local:
        s = jnp.concatenate([s[0:blk] + jnp.concatenate(prev_bias, axis=1), s[blk:2 * blk],
                             s[2 * blk:3 * blk] + jnp.concatenate(next_bias, axis=1), s[3 * blk:]], axis=0)
    sk = jnp.concatenate([sk] * nsub, axis=1)
    m = jnp.maximum(jnp.max(s, axis=0, keepdims=True), sk)
    e = jnp.exp2(s - m).astype(BF16)
    extra = jnp.exp2(sk - m)
    for i in range(nsub):
        r = _mm(values[i], e[:, i * nq:(i + 1) * nq])
        den = r[2 * hd:2 * hd + 1, :] + extra[:, i * nq:(i + 1) * nq]
        for p in range(ATT_Q_HEADS):
            c = p % 2
            head = p // 2 + 4 * c
            cols = slice(p * blk, (p + 1) * blk)
            o_ref[0, head * hd:(head + 1) * hd, i * blk:(i + 1) * blk] = (
                r[c * hd:(c + 1) * hd, cols] / den[:, cols]).astype(BF16)


def _attention(aq, ak, avt, ak_ctx, avt_ctx, sink, *, layer, local):
    bsz, seq, _ = aq.shape
    nb = seq // ATT_BLOCK
    nsub = min(ATT_SUB, nb)
    tq = nsub * ATT_BLOCK
    lc = ak_ctx.shape[1]
    qspec = pl.BlockSpec((1, tq, BRANCH_W), lambda b, n: (b, n, 0))
    kx_spec = pl.BlockSpec((1, lc, LANES), lambda b, n: (b, 0, 0))
    vx_spec = pl.BlockSpec((1, LANES, lc), lambda b, n: (b, 0, 0))
    sink_spec = pl.BlockSpec((1, 1, ATT_Q_HEADS * ATT_BLOCK), lambda b, n: (layer, 0, 0))
    if local:
        prev = lambda n: jnp.maximum(n * nsub - 1, 0)
        nxt = lambda n: jnp.minimum((n + 1) * nsub, nb - 1)
        kspecs = [pl.BlockSpec((1, ATT_BLOCK, LANES), lambda b, n: (b, prev(n), 0)),
                  pl.BlockSpec((1, tq, LANES), lambda b, n: (b, n, 0)),
                  pl.BlockSpec((1, ATT_BLOCK, LANES), lambda b, n: (b, nxt(n), 0))]
        vspecs = [pl.BlockSpec((1, LANES, ATT_BLOCK), lambda b, n: (b, 0, prev(n))),
                  pl.BlockSpec((1, LANES, tq), lambda b, n: (b, 0, n)),
                  pl.BlockSpec((1, LANES, ATT_BLOCK), lambda b, n: (b, 0, nxt(n)))]
        in_specs = [qspec, *kspecs, kx_spec, *vspecs, vx_spec, sink_spec]
        args = (aq, ak, ak, ak, ak_ctx, avt, avt, avt, avt_ctx, sink)
    else:
        in_specs = [qspec, kx_spec, vx_spec, sink_spec]
        args = (aq, ak_ctx, avt_ctx, sink)
    return pl.pallas_call(
        functools.partial(_attn_kernel, nb=nb, local=local),
        grid=(bsz, nb // nsub),
        in_specs=in_specs,
        out_specs=pl.BlockSpec((1, BRANCH_W, tq), lambda b, n: (b, 0, n)),
        out_shape=jax.ShapeDtypeStruct((bsz, BRANCH_W, seq), BF16),
        compiler_params=_params("parallel", "parallel"),
        name="attention",
    )(*args)


def _merge_kernel(x_ref, mod_ref, g_ref, hf_ref, hb_ref, og_ref, yb_ref, att_ref, yd_ref, wg_ref, wb_ref, wo_ref,
                  o_ref):
    d = D_MODEL
    tm = x_ref.shape[1]
    parts = MERGE_PARTS if tm % (MERGE_PARTS * 2 * LANES) == 0 else 1
    rows = tm // parts
    for part in range(parts):
        tok = slice(part * rows, (part + 1) * rows)
        x = x_ref[0, tok, :]
        u = (_rms(x) * g_ref[0, 0:1, :] * (1.0 + mod_ref[0, 0, 1:2, :]) + mod_ref[0, 0, 0:1, :]).astype(BF16)
        ya_t = (og_ref[0, :, tok].astype(F32)
                * (hf_ref[0, :, tok].astype(F32) + hb_ref[0, :, tok].astype(F32))).astype(BF16)
        ys = (ya_t, yb_ref[0, tok, :], att_ref[0, :, tok], yd_ref[0, tok, :])
        feature_major = (True, False, True, False)
        merged = None
        for n in range(N_BRANCH):
            gate = _sigmoid(_mm(u, wg_ref[0, :, n * d:(n + 1) * d]))
            proj = _tn(ys[n], wb_ref[0, n]) if feature_major[n] else _mm(ys[n], wb_ref[0, n])
            term = gate * proj
            merged = term if merged is None else merged + term
        out = _mm(merged.astype(BF16), wo_ref[0])
        o_ref[0, tok, :] = x + mod_ref[0, 0, 2:3, :] * (_rms(out) * g_ref[0, 1:2, :])


def _merge(x, mod, norm_g, hf, hb, og, yb, att, yd, w_mg, w_branch, w_out, *, layer, mod_row, tm):
    bsz, seq, d = x.shape
    mrow = (lambda b: b) if mod_row is None else (lambda b: mod_row)
    tok = lambda w: pl.BlockSpec((1, tm, w), lambda b, i: (b, i, 0))
    feat = pl.BlockSpec((1, BRANCH_W, tm), lambda b, i: (b, 0, i))
    lsel3 = lambda b, i: (layer, 0, 0)
    return pl.pallas_call(
        _merge_kernel,
        grid=(bsz, seq // tm),
        in_specs=[tok(d),
                  pl.BlockSpec((1, 1, 6, d), lambda b, i: (layer, mrow(b), 0, 0)),
                  _const_spec((1, 4, d), lsel3),
                  feat, feat, feat, tok(BRANCH_W), feat, tok(BRANCH_W),
                  _const_spec((1, d, N_BRANCH * d), lsel3),
                  _const_spec((1, N_BRANCH, BRANCH_W, d), lambda b, i: (layer, 0, 0, 0)),
                  _const_spec((1, d, d), lsel3)],
        out_specs=tok(d),
        out_shape=jax.ShapeDtypeStruct((bsz, seq, d), F32),
        compiler_params=_params("parallel", "parallel"),
        name="merge",
    )(x, mod, norm_g, hf, hb, og, yb, att, yd, w_mg, w_branch, w_out)


def _ffn_kernel(xm_ref, xp_ref, xn_ref, mod_ref, g_ref, wu_ref, wc_ref, wd_ref, o_ref, u_scr, rg_scr, rv_scr,
                h_scr, *, tm, seq):
    u = _modulated_tile(xm_ref, xp_ref, xn_ref, g_ref[0, 2:3, :], mod_ref[0, 0, 3:4, :], mod_ref[0, 0, 4:5, :],
                        tm, seq)
    u_scr[...] = u.astype(BF16)
    cw = FF_CHUNK
    for c in range(D_FF // cw):
        gcols = slice(c * cw, (c + 1) * cw)
        vcols = slice(D_FF + c * cw, D_FF + (c + 1) * cw)
        rg_scr[...] = _mm(u_scr[...], wu_ref[0, :, gcols])
        rv_scr[...] = _mm(u_scr[...], wu_ref[0, :, vcols])
        half = _conv3_taps(rg_scr, 0.5 * wc_ref[0, :, gcols], tm)
        val = _conv3(rv_scr, wc_ref, vcols, tm)
        h_scr[:, gcols] = ((half + half * jnp.tanh(half)) * val).astype(BF16)
    out = _mm(h_scr[...], wd_ref[0])
    x = xm_ref[0]
    o_ref[0] = x + mod_ref[0, 0, 5:6, :] * (_rms(out) * g_ref[0, 3:4, :])


def _ffn(x, mod, norm_g, w_up, w_conv, w_down, *, layer, mod_row, tm):
    bsz, seq, d = x.shape
    main, prev, nxt = _halo_specs(tm, d, seq)
    mrow = (lambda b: b) if mod_row is None else (lambda b: mod_row)
    lsel3 = lambda b, i: (layer, 0, 0)
    return pl.pallas_call(
        functools.partial(_ffn_kernel, tm=tm, seq=seq),
        grid=(bsz, seq // tm),
        in_specs=[main, prev, nxt,
                  pl.BlockSpec((1, 1, 6, d), lambda b, i: (layer, mrow(b), 0, 0)),
                  _const_spec((1, 4, d), lsel3),
                  _const_spec((1, d, 2 * D_FF), lsel3),
                  _const_spec((1, 3, 2 * D_FF), lsel3),
                  _const_spec((1, D_FF, d), lsel3)],
        out_specs=main,
        out_shape=jax.ShapeDtypeStruct((bsz, seq, d), F32),
        scratch_shapes=[pltpu.VMEM((tm + 2 * HALO, d), BF16), pltpu.VMEM((tm + 2 * HALO, FF_CHUNK), F32),
                        pltpu.VMEM((tm + 2 * HALO, FF_CHUNK), F32), pltpu.VMEM((tm, D_FF), BF16)],
        compiler_params=_params("parallel", "parallel"),
        name="convffn",
    )(x, x, x, mod, norm_g, w_up, w_conv, w_down)


def _rope_tables(seq):
    t = jnp.arange(seq, dtype=jnp.int32)[:, None]
    lane = jnp.arange(LANES, dtype=jnp.int32)[None, :]
    in_head = lane % ATT_HEAD_DIM
    n_freq = ATT_HEAD_DIM // 4
    inv_freq = ROPE_BASE ** (-(in_head % n_freq).astype(F32) / n_freq)
    pos = jnp.where(in_head < ATT_HEAD_DIM // 2, t // GRID_W, t % GRID_W).astype(F32)
    ang = pos * inv_freq
    sign = jnp.where((in_head % (2 * n_freq)) < n_freq, -1.0, 1.0)
    return jnp.cos(ang), sign * jnp.sin(ang)


def _tile_rows(seq):
    return min(seq, 512)


def kernel(x, c, ctx, c_ctx, w_mod, b_mod, norm_g, w_in, mlstm_qk_conv, mlstm_gate_bias, pool_w, pool_scale,
           attn_sink, sconv_w, w_branch, w_out, ffn_up, ffn_conv, ffn_down):
    bsz, seq, d = x.shape
    lc = ctx.shape[1]
    depth = w_mod.shape[0]
    assert d == D_MODEL and seq % 512 == 0 and lc % 256 == 0 and bsz + 1 <= 8

    cond = jnp.concatenate([c, c_ctx[None, :], jnp.zeros((8 - bsz - 1, d), F32)], axis=0)
    mod = _modulation(cond, w_mod, b_mod).reshape(depth, 8, 6, d)

    g0 = BRANCH_W * 4
    g1 = g0 + 4 * MLSTM_HEADS
    mg0 = w_in.shape[-1] - N_BRANCH * d
    p0 = g1
    aq0 = p0 + BRANCH_W
    ak0 = aq0 + BRANCH_W
    av0 = ak0 + ATT_KV_HEADS * ATT_HEAD_DIM
    sc0 = av0 + ATT_KV_HEADS * ATT_HEAD_DIM
    w_aq = w_in[:, :, aq0:ak0].reshape(depth, d, ATT_KV_HEADS, ATT_Q_HEADS // ATT_KV_HEADS, ATT_HEAD_DIM)
    w_aq = jnp.swapaxes(w_aq, 2, 3).reshape(depth, d, BRANCH_W)
    v0 = 2 * BRANCH_W
    w1 = jnp.concatenate([w_in[:, :, :v0], w_in[:, :, p0:aq0], w_aq, w_in[:, :, ak0:av0], w_in[:, :, sc0:mg0]],
                         axis=-1).astype(BF16)
    perm = jnp.array(_GATE_PERM)
    wt = jnp.concatenate([w_in[:, :, v0:g0], w_in[:, :, av0:sc0], w_in[:, :, g0:g1][:, :, perm]], axis=-1)
    wt = jnp.swapaxes(wt, 1, 2).astype(BF16)
    gb = mlstm_gate_bias.reshape(depth, 4 * MLSTM_HEADS)[:, perm][:, :, None]
    w_mg = w_in[:, :, mg0:].astype(BF16)
    pw = pool_w.astype(BF16)
    ps = pool_scale[:, None, :]
    sink_heads = attn_sink.reshape(depth, ATT_KV_HEADS, ATT_Q_HEADS // ATT_KV_HEADS).swapaxes(1, 2)
    sink = jnp.broadcast_to(sink_heads.reshape(depth, 1, ATT_Q_HEADS, 1),
                            (depth, 1, ATT_Q_HEADS, ATT_BLOCK)).reshape(depth, 1, ATT_Q_HEADS * ATT_BLOCK)
    wb = w_branch.astype(BF16)
    wo = w_out.astype(BF16)
    wu = ffn_up.astype(BF16)
    wc = ffn_conv
    wd = ffn_down.astype(BF16)

    cos_x, sin_x = _rope_tables(seq)
    cos_c, sin_c = jnp.ones((lc, LANES), F32), jnp.zeros((lc, LANES), F32)

    hd = MLSTM_HD
    zero_c = jnp.zeros((bsz, MLSTM_HEADS, 2 * hd, hd), F32)
    zero_m = jnp.zeros((bsz, 8, LANES), F32)
    tm_x, tm_c = _tile_rows(seq), _tile_rows(lc)
    scan = functools.partial(_mlstm, chunk=MLSTM_CHUNK)

    h = ctx
    for l in range(depth):
        ctx_out = l < depth - 1
        proj = functools.partial(_inproj, mod=mod, norm_g=norm_g, w1=w1, wt=wt, gb=gb, qkc=mlstm_qk_conv,
                                 scw=sconv_w, pw=pw, ps=ps, layer=l)
        qk_c, v_c, og_c, gt_c, yb_c, aq_c, ak_c, avt_c, yd_c = proj(h, cos=cos_c, sin=sin_c, mod_row=bsz, tm=tm_c)
        qk_x, v_x, og_x, gt_x, yb_x, aq_x, ak_x, avt_x, yd_x = proj(x, cos=cos_x, sin=sin_x, mod_row=None, tm=tm_x)
        hf_c, cf, mf = scan(qk_c, v_c, gt_c, zero_c, zero_m, reverse=False)
        hb_c, cb, mb = scan(qk_c, v_c, gt_c, zero_c, zero_m, reverse=True)
        hf_x, _, _ = scan(qk_x, v_x, gt_x, cf, mf, reverse=False)
        hb_x, _, _ = scan(qk_x, v_x, gt_x, cb, mb, reverse=True)
        att_x = _attention(aq_x, ak_x, avt_x, ak_c, avt_c, sink, layer=l, local=True)
        tail = dict(norm_g=norm_g, layer=l)
        x = _merge(x, mod, hf=hf_x, hb=hb_x, og=og_x, yb=yb_x, att=att_x, yd=yd_x, w_mg=w_mg, w_branch=wb,
                   w_out=wo, mod_row=None, tm=tm_x, **tail)
        x = _ffn(x, mod, w_up=wu, w_conv=wc, w_down=wd, mod_row=None, tm=tm_x, **tail)
        if ctx_out:
            att_c = _attention(aq_c, ak_c, avt_c, ak_c, avt_c, sink, layer=l, local=False)
            h = _merge(h, mod, hf=hf_c, hb=hb_c, og=og_c, yb=yb_c, att=att_c, yd=yd_c, w_mg=w_mg, w_branch=wb,
                       w_out=wo, mod_row=bsz, tm=tm_c, **tail)
            h = _ffn(h, mod, w_up=wu, w_conv=wc, w_down=wd, mod_row=bsz, tm=tm_c, **tail)
    return x
```

```python
import functools

import jax
import jax.numpy as jnp
from jax import lax
from jax.experimental import pallas as pl
from jax.experimental.pallas import tpu as pltpu

F32 = jnp.float32
BF16 = jnp.bfloat16
HIGHEST = lax.Precision.HIGHEST

D_MODEL = 1024
GRID_W = 64
N_BRANCH = 4
BRANCH_W = D_MODEL // 2
MLSTM_HEADS = 4
MLSTM_HD = BRANCH_W // MLSTM_HEADS
MLSTM_CHUNK = 256
MLSTM_CHUNKS_PER_STEP = 8
POOL_HALF = (1, 2, 4, 8)
POOL_GW = BRANCH_W // len(POOL_HALF)
ATT_HEAD_DIM = 64
ATT_Q_HEADS = BRANCH_W // ATT_HEAD_DIM
ATT_KV_HEADS = 2
ATT_BLOCK = 128
ATT_SUB = 8
LOG2E = 1.4426950408889634
ATT_Q_SCALE = ATT_HEAD_DIM ** -0.5 * LOG2E
ROPE_BASE = 10000.0
D_FF = 2816
EPS = 1e-6

LANES = 128
HALO = 8
FF_CHUNK = 256
MERGE_PARTS = 2
VMEM_LIMIT = 56 * 1024 * 1024

_OFF = {}
_acc = 0
for _name, _w in (("q", 512), ("k", 512), ("pool", 512), ("aq", 512), ("ak", 128), ("sb", 512), ("sc", 512),
                  ("sx", 512)):
    _OFF[_name] = _acc
    _acc += _w
W1_COLS = _acc
_TOFF = {}
_acc = 0
for _name, _w in (("v", 512), ("o", 512), ("av", 128), ("gates", 16)):
    _TOFF[_name] = _acc
    _acc += _w
WT_ROWS = _acc
_GATE_PERM = (0, 1, 2, 3, 8, 9, 10, 11, 4, 5, 6, 7, 12, 13, 14, 15)


def _iota(shape, dim):
    return lax.broadcasted_iota(jnp.int32, shape, dim)


def _rms(x):
    return x * lax.rsqrt(jnp.mean(x * x, axis=-1, keepdims=True) + EPS)


def _sigmoid(x):
    return 1.0 / (1.0 + jnp.exp(-x))


def _nt(a, b, **kw):
    return lax.dot_general(a, b, (((1,), (1,)), ((), ())), preferred_element_type=F32, **kw)


def _tn(a, b, **kw):
    return lax.dot_general(a, b, (((0,), (0,)), ((), ())), preferred_element_type=F32, **kw)


def _mm(a, b, **kw):
    return jnp.dot(a, b, preferred_element_type=F32, **kw)


def _params(*sem):
    return pltpu.CompilerParams(dimension_semantics=sem, vmem_limit_bytes=VMEM_LIMIT)


def _const_spec(shape, index_map):
    return pl.BlockSpec(shape, index_map, pipeline_mode=pl.Buffered(1))


def _mod_kernel(c_ref, w_ref, b_ref, o_ref):
    cond = c_ref[...]
    sc = cond * _sigmoid(cond)
    o_ref[0] = _mm(sc, w_ref[0], precision=HIGHEST) + b_ref[0]


def _modulation(cond, w_mod, b_mod):
    depth, d, n = w_mod.shape
    rows = cond.shape[0]
    tn = d
    return pl.pallas_call(
        _mod_kernel,
        grid=(depth, n // tn),
        in_specs=[pl.BlockSpec((rows, d), lambda l, j: (0, 0)),
                  pl.BlockSpec((1, d, tn), lambda l, j: (l, 0, j)),
                  pl.BlockSpec((1, 1, tn), lambda l, j: (l, 0, j))],
        out_specs=pl.BlockSpec((1, rows, tn), lambda l, j: (l, 0, j)),
        out_shape=jax.ShapeDtypeStruct((depth, rows, n), F32),
        compiler_params=_params("parallel", "parallel"),
        name="modulation",
    )(cond, w_mod, b_mod.reshape(depth, 1, n))


def _halo_specs(tm, d, seq):
    nh = seq // HALO
    per = tm // HALO
    main = pl.BlockSpec((1, tm, d), lambda b, i: (b, i, 0))
    prev = pl.BlockSpec((1, HALO, d), lambda b, i: (b, jnp.maximum(i * per - 1, 0), 0))
    nxt = pl.BlockSpec((1, HALO, d), lambda b, i: (b, jnp.minimum((i + 1) * per, nh - 1), 0))
    return main, prev, nxt


def _modulated_tile(xm_ref, xp_ref, xn_ref, gain, shift, scale, tm, seq):
    i = pl.program_id(1)
    mult = gain * (1.0 + scale)
    mod = lambda x: _rms(x) * mult + shift
    before = jnp.where(i > 0, mod(xp_ref[0]), 0.0)
    after = jnp.where((i + 1) * tm < seq, mod(xn_ref[0]), 0.0)
    return jnp.concatenate([before, mod(xm_ref[0]), after], axis=0)


def _conv3_taps(r_ref, w, tm):
    return (r_ref[HALO - 1:HALO - 1 + tm, :] * w[0:1] + r_ref[HALO:HALO + tm, :] * w[1:2]
            + r_ref[HALO + 1:HALO + 1 + tm, :] * w[2:3])


def _conv3(r_ref, w_ref, cols, tm):
    return _conv3_taps(r_ref, w_ref[0, :, cols], tm)


def _rope(r, cos, sin):
    lane = _iota((1, LANES), 1)
    first = (lane % 32) < 16
    swapped = jnp.where(first, pltpu.roll(r, LANES - 16, 1), pltpu.roll(r, 16, 1))
    return r * cos + swapped * sin


def _inproj_kernel(xm_ref, xp_ref, xn_ref, mod_ref, g_ref, w_ref, wt_ref, gb_ref, qkc_ref, scw_ref, pw_ref,
                   ps_ref, cos_ref, sin_ref,
                   qk_out, vt_out, ogt_out, gates_out, yb_out, aq_out, ak_out, avt_out, yd_out,
                   u_scr, um_scr, r_scr, r2_scr, *, tm, seq):
    i = pl.program_id(1)
    u = _modulated_tile(xm_ref, xp_ref, xn_ref, g_ref[0, 0:1, :], mod_ref[0, 0, 0:1, :], mod_ref[0, 0, 1:2, :],
                        tm, seq)
    u_scr[...] = u.astype(BF16)
    um_scr[...] = u[HALO:HALO + tm].astype(BF16)

    def wcols(name, width):
        o = _OFF[name]
        return w_ref[0, :, o:o + width]

    for c, name in enumerate(("q", "k")):
        cols = slice(c * BRANCH_W, (c + 1) * BRANCH_W)
        scr = r_scr if name == "q" else r2_scr
        scr[...] = _mm(u_scr[...], wcols(name, BRANCH_W))
        if name == "k":
            r_scr[...] = _mm(u_scr[...], wcols("pool", BRANCH_W))
        half = _conv3_taps(scr, 0.5 * qkc_ref[0, :, cols], tm)
        a = half + half * jnp.tanh(half)
        if name == "k":
            a = a * MLSTM_HD ** -0.5
        qk_out[0, :, cols] = a.astype(BF16)

    um = um_scr[...]

    def trows(name, height):
        o = _TOFF[name]
        return wt_ref[0, o:o + height, :]

    g = _nt(trows("gates", 16), um) + gb_ref[0]
    logsig = jnp.minimum(g, 0.0) - jnp.log(1.0 + jnp.exp(-jnp.abs(g)))
    is_forget = (_iota((16, 1), 0) % 8) >= MLSTM_HEADS
    g = jnp.where(is_forget, logsig, g) * LOG2E
    gates_out[0, 0] = g[0:8]
    gates_out[0, 1] = g[8:16]

    assert max(POOL_HALF) <= HALO
    t_top = i * tm + _iota((HALO, 1), 0)
    t_bot = t_top + (tm - HALO)
    for gi, half in enumerate(POOL_HALF):
        cols = slice(gi * POOL_GW, (gi + 1) * POOL_GW)
        acc = r_scr[HALO - half:HALO - half + tm, cols]
        for o in range(-half + 1, half):
            acc = acc + r_scr[HALO + o:HALO + o + tm, cols]
        inv = lambda t: 1.0 / (jnp.minimum(t + half, seq) - jnp.maximum(t - half, 0)).astype(F32)
        mean = jnp.concatenate([acc[:HALO] * inv(t_top), acc[HALO:tm - HALO] * (0.5 / half),
                                acc[tm - HALO:] * inv(t_bot)], axis=0)
        delta = mean - r_scr[HALO:HALO + tm, cols]
        y = _mm(delta.astype(BF16), pw_ref[0, gi]) * ps_ref[0, 0:1, cols]
        yb_out[0, :, cols] = y.astype(BF16)
        if gi == 0:
            rb = _mm(um, wcols("sb", BRANCH_W))
        elif gi == 1:
            rc = _mm(u_scr[...], wcols("sc", BRANCH_W))
        elif gi == 2:
            r2_scr[...] = rc * _mm(u_scr[...], wcols("sx", BRANCH_W))
        else:
            rq = _mm(um, wcols("aq", BRANCH_W))

    cos = cos_ref[...]
    sin = sin_ref[...]
    for j in range(BRANCH_W // LANES):
        cols = slice(j * LANES, (j + 1) * LANES)
        aq_out[0, :, cols] = (_rope(rq[:, cols], cos, sin) * ATT_Q_SCALE).astype(BF16)
        if j == 0:
            vt_out[0] = _nt(trows("v", BRANCH_W), um).astype(BF16)
        elif j == 2:
            ogt_out[0] = _sigmoid(_nt(trows("o", BRANCH_W), um)).astype(BF16)
    avt_out[0] = _nt(trows("av", LANES), um).astype(BF16)
    ak_out[0] = _rope(_mm(um, wcols("ak", LANES)), cos, sin).astype(BF16)

    yd_out[0] = (rb * _conv3(r2_scr, scw_ref, slice(0, BRANCH_W), tm)).astype(BF16)


def _inproj(x, mod, norm_g, w1, wt, gb, qkc, scw, pw, ps, cos, sin, *, layer, mod_row, tm):
    bsz, seq, d = x.shape
    nt = seq // tm
    main, prev, nxt = _halo_specs(tm, d, seq)
    mrow = (lambda b: b) if mod_row is None else (lambda b: mod_row)
    lsel3 = lambda b, i: (layer, 0, 0)
    in_specs = [
        main, prev, nxt,
        pl.BlockSpec((1, 1, 6, d), lambda b, i: (layer, mrow(b), 0, 0)),
        _const_spec((1, 4, d), lsel3),
        _const_spec((1, d, W1_COLS), lsel3),
        _const_spec((1, WT_ROWS, d), lsel3),
        _const_spec((1, 16, 1), lsel3),
        _const_spec((1, 3, 2 * BRANCH_W), lsel3),
        _const_spec((1, 3, BRANCH_W), lsel3),
        _const_spec((1, len(POOL_HALF), POOL_GW, POOL_GW), lambda b, i: (layer, 0, 0, 0)),
        _const_spec((1, 1, BRANCH_W), lsel3),
        pl.BlockSpec((tm, LANES), lambda b, i: (i, 0)),
        pl.BlockSpec((tm, LANES), lambda b, i: (i, 0)),
    ]
    tok = lambda w: pl.BlockSpec((1, tm, w), lambda b, i: (b, i, 0))
    feat = lambda w: pl.BlockSpec((1, w, tm), lambda b, i: (b, 0, i))
    out_specs = [tok(2 * BRANCH_W), feat(BRANCH_W), feat(BRANCH_W),
                 pl.BlockSpec((1, 2, 8, tm), lambda b, i: (b, 0, 0, i)),
                 tok(BRANCH_W), tok(BRANCH_W), tok(LANES), feat(LANES), tok(BRANCH_W)]
    act = lambda w: jax.ShapeDtypeStruct((bsz, seq, w), BF16)
    act_t = lambda w: jax.ShapeDtypeStruct((bsz, w, seq), BF16)
    out_shape = [act(2 * BRANCH_W), act_t(BRANCH_W), act_t(BRANCH_W),
                 jax.ShapeDtypeStruct((bsz, 2, 8, seq), F32),
                 act(BRANCH_W), act(BRANCH_W), act(LANES), act_t(LANES), act(BRANCH_W)]
    return pl.pallas_call(
        functools.partial(_inproj_kernel, tm=tm, seq=seq),
        grid=(bsz, nt),
        in_specs=in_specs,
        out_specs=out_specs,
        out_shape=out_shape,
        scratch_shapes=[pltpu.VMEM((tm + 2 * HALO, d), BF16), pltpu.VMEM((tm, d), BF16),
                        pltpu.VMEM((tm + 2 * HALO, BRANCH_W), F32), pltpu.VMEM((tm + 2 * HALO, BRANCH_W), F32)],
        compiler_params=_params("parallel", "parallel"),
        name="inproj",
    )(x, x, x, mod, norm_g, w1, wt, gb, qkc, scw, pw, ps, cos, sin)


def _split3(x):
    x = jnp.concatenate([x, jnp.zeros_like(x)], axis=0)
    hi = x.astype(BF16)
    r1 = x - hi.astype(F32)
    mid = r1.astype(BF16)
    lo = (r1 - mid.astype(F32)).astype(BF16)
    return jnp.concatenate([hi, mid, lo], axis=0)


def _sum3(y):
    return (y[0:8] + y[16:24]) + y[32:40]


def _mlstm_kernel(qk_ref, vt_ref, g_ref, c0_ref, m0_ref, h_out, c_out, m_out, *, chunk, reverse):
    L = chunk
    hd = MLSTM_HD
    nck = qk_ref.shape[1] // L

    @pl.when(pl.program_id(1) == 0)
    def _():
        c_out[...] = c0_ref[...]
        m_out[...] = m0_ref[...]

    s_idx = _iota((L, L), 0)
    t_idx = _iota((L, L), 1)
    seen = (s_idx >= t_idx) if reverse else (s_idx <= t_idx)
    seen_bf = seen.astype(BF16)
    is_last = _iota((1, L), 1) == (0 if reverse else L - 1)
    ones = jnp.ones((hd, L), BF16)
    neg_inf = -jnp.inf

    heads = range(MLSTM_HEADS)
    wide = lambda parts: jnp.concatenate(parts, axis=1)
    toks = [slice(((nck - 1 - i) if reverse else i) * L, (((nck - 1 - i) if reverse else i) + 1) * L)
            for i in range(nck)]

    qs, ks, vaugs, b_rows, ig_rows, keys, scores = [], [], [], [], [], [], []
    for tok in toks:
        g = g_ref[0, 0, :, tok]
        g3 = _split3(g)
        cum_row = _sum3(_mm(g3, seen_bf))
        key_rows = cum_row[MLSTM_HEADS:] - g[:MLSTM_HEADS]
        key_cols = jnp.transpose(jnp.concatenate([key_rows, jnp.zeros_like(key_rows)], axis=0))
        for h in heads:
            qs.append(qk_ref[0, tok, h * hd:(h + 1) * hd])
            ks.append(qk_ref[0, tok, BRANCH_W + h * hd:BRANCH_W + (h + 1) * hd])
            vaugs.append(jnp.concatenate([vt_ref[0, h * hd:(h + 1) * hd, tok], ones], axis=0))
            b_rows.append(cum_row[MLSTM_HEADS + h:MLSTM_HEADS + h + 1, :])
            ig_rows.append(g[h:h + 1, :])
            keys.append(jnp.broadcast_to(key_cols[:, h:h + 1], (L, L)))
            scores.append(_nt(ks[-1], qs[-1]))
    n_pairs = len(qs)
    b_row = wide(b_rows)
    s_w = _iota((L, n_pairs * L), 0)
    t_w = _iota((L, n_pairs * L), 1) % L
    seen_w = (s_w >= t_w) if reverse else (s_w <= t_w)
    dmat = jnp.where(seen_w, b_row - wide(keys), neg_inf)
    m_intra = jnp.max(dmat, axis=0, keepdims=True)
    p = (wide(scores) * jnp.exp2(dmat - m_intra)).astype(BF16)
    b_tots = [jnp.max(jnp.where(is_last, b, neg_inf), axis=1, keepdims=True) for b in b_rows]
    u_loc = wide([bt - b + ig for bt, b, ig in zip(b_tots, b_rows, ig_rows)])
    m_locs = [jnp.max(u_loc[:, e * L:(e + 1) * L], axis=1, keepdims=True) for e in range(n_pairs)]
    wk_loc = jnp.exp2(u_loc - wide([jnp.broadcast_to(m, (1, L)) for m in m_locs]))

    def chunk_products(i):
        pairs = range(i * MLSTM_HEADS, (i + 1) * MLSTM_HEADS)
        r = {e: _mm(vaugs[e], p[:, e * L:(e + 1) * L]) for e in pairs}
        kv = {e: _mm((vaugs[e].astype(F32) * wk_loc[:, e * L:(e + 1) * L]).astype(BF16), ks[e])
              for e in pairs}
        return r, kv

    ready = chunk_products(0)
    for i, tok in enumerate(toks):
        r_intra, kv_loc = ready
        if i + 1 < nck:
            ready = chunk_products(i + 1)
        for h in heads:
            e = i * MLSTM_HEADS + h
            lanes = slice(e * L, (e + 1) * L)
            m_prev = m_out[0, h:h + 1, 0:1]
            c_prev = c_out[0, h]
            a = b_rows[e] + m_prev
            mt = jnp.maximum(a, m_intra[:, lanes])
            r = (jnp.exp2(m_intra[:, lanes] - mt) * r_intra[e]
                 + jnp.exp2(a - mt) * _nt(c_prev.astype(BF16), qs[e]))
            den = jnp.maximum(jnp.abs(r[hd:]), jnp.exp2(-mt))
            h_out[0, h * hd:(h + 1) * hd, tok] = (r[:hd] / den).astype(BF16)
            m_new = jnp.max(jnp.where(is_last, mt, neg_inf), axis=1, keepdims=True)
            c_out[0, h] = (jnp.exp2(b_tots[e] + m_prev - m_new) * c_prev
                           + jnp.exp2(m_locs[e] - m_new) * kv_loc[e])
            m_out[0, h:h + 1, :] = jnp.broadcast_to(m_new, (1, LANES))


def _mlstm(qk, vt, gates, c0, m0, *, chunk, reverse):
    bsz, seq, _ = qk.shape
    nck = min(MLSTM_CHUNKS_PER_STEP, seq // chunk)
    tb = nck * chunk
    ns = seq // tb
    hd = MLSTM_HD
    direction = 1 if reverse else 0
    blk = (lambda j: ns - 1 - j) if reverse else (lambda j: j)
    state_c = pl.BlockSpec((1, MLSTM_HEADS, 2 * hd, hd), lambda b, j: (b, 0, 0, 0))
    state_m = pl.BlockSpec((1, 8, LANES), lambda b, j: (b, 0, 0))
    return pl.pallas_call(
        functools.partial(_mlstm_kernel, chunk=chunk, reverse=reverse),
        grid=(bsz, ns),
        in_specs=[pl.BlockSpec((1, tb, 2 * BRANCH_W), lambda b, j: (b, blk(j), 0)),
                  pl.BlockSpec((1, BRANCH_W, tb), lambda b, j: (b, 0, blk(j))),
                  pl.BlockSpec((1, 1, 8, tb), lambda b, j: (b, direction, 0, blk(j))),
                  state_c, state_m],
        out_specs=[pl.BlockSpec((1, BRANCH_W, tb), lambda b, j: (b, 0, blk(j))),
                   state_c, state_m],
        out_shape=[jax.ShapeDtypeStruct((bsz, BRANCH_W, seq), BF16),
                   jax.ShapeDtypeStruct((bsz, MLSTM_HEADS, 2 * hd, hd), F32),
                   jax.ShapeDtypeStruct((bsz, 8, LANES), F32)],
        compiler_params=_params("parallel", "arbitrary"),
        name="mlstm_bwd" if reverse else "mlstm_fwd",
    )(qk, vt, gates, c0, m0)


def _attn_kernel(*refs, nb, local):
    blk = ATT_BLOCK
    hd = ATT_HEAD_DIM
    if local:
        q_ref, kp_ref, kc_ref, kn_ref, kx_ref, vp_ref, vc_ref, vn_ref, vx_ref, sink_ref, o_ref = refs
        k_loc = jnp.concatenate([kp_ref[0], kc_ref[0], kn_ref[0]], axis=0)
        vt_loc = jnp.concatenate([vp_ref[0], vc_ref[0], vn_ref[0]], axis=1)
    else:
        q_ref, kx_ref, vx_ref, sink_ref, o_ref = refs
    nsub = q_ref.shape[1] // blk
    step = pl.program_id(1)
    lo = _iota((1, LANES), 1) < hd
    nq = ATT_Q_HEADS * blk
    sk = sink_ref[0] * LOG2E
    if local:
        kj = _iota((blk, nq), 0)
        qi = _iota((blk, nq), 1) % blk
        after = kj >= qi
        before = kj <= qi
        band_prev = jnp.where(after, 0.0, -jnp.inf)
        band_next = jnp.where(before, 0.0, -jnp.inf)

    scores, values, prev_bias, next_bias = [], [], [], []
    for i in range(nsub):
        if local:
            k = jnp.concatenate([k_loc[i * blk:(i + 3) * blk], kx_ref[0]], axis=0)
            vt = jnp.concatenate([vt_loc[:, i * blk:(i + 3) * blk], vx_ref[0]], axis=1)
            n = step * nsub + i
            prev_bias.append(band_prev if i > 0 else jnp.where(after & (n > 0), 0.0, -jnp.inf))
            next_bias.append(band_next if i < nsub - 1 else jnp.where(before & (n < nb - 1), 0.0, -jnp.inf))
        else:
            k = kx_ref[0]
            vt = vx_ref[0]
        nk = k.shape[0]
        values.append(jnp.concatenate([vt, jnp.ones((16, nk), BF16)], axis=0))
        parts = []
        for j in range(BRANCH_W // LANES):
            q2 = q_ref[0, i * blk:(i + 1) * blk, j * LANES:(j + 1) * LANES]
            zero = jnp.zeros_like(q2)
            parts += [jnp.where(lo, q2, zero), jnp.where(lo, zero, q2)]
        scores.append(_nt(k, jnp.concatenate(parts, axis=0)))
    s = jnp.concatenate(scores, axis=1)
    if local:
        s = jnp.concatenate([s[0:blk] + jnp.concatenate(prev_bias, axis=1), s[blk:2 * blk],
                             s[2 * blk:3 * blk] + jnp.concatenate(next_bias, axis=1), s[3 * blk:]], axis=0)
    sk = jnp.concatenate([sk] * nsub, axis=1)
    m = jnp.maximum(jnp.max(s, axis=0, keepdims=True), sk)
    e = jnp.exp2(s - m).astype(BF16)
    extra = jnp.exp2(sk - m)
    for i in range(nsub):
        r = _mm(values[i], e[:, i * nq:(i + 1) * nq])
        den = r[2 * hd:2 * hd + 1, :] + extra[:, i * nq:(i + 1) * nq]
        for p in range(ATT_Q_HEADS):
            c = p % 2
            head = p // 2 + 4 * c
            cols = slice(p * blk, (p + 1) * blk)
            o_ref[0, head * hd:(head + 1) * hd, i * blk:(i + 1) * blk] = (
                r[c * hd:(c + 1) * hd, cols] / den[:, cols]).astype(BF16)


def _attention(aq, ak, avt, ak_ctx, avt_ctx, sink, *, layer, local):
    bsz, seq, _ = aq.shape
    nb = seq // ATT_BLOCK
    nsub = min(ATT_SUB, nb)
    tq = nsub * ATT_BLOCK
    lc = ak_ctx.shape[1]
    qspec = pl.BlockSpec((1, tq, BRANCH_W), lambda b, n: (b, n, 0))
    kx_spec = pl.BlockSpec((1, lc, LANES), lambda b, n: (b, 0, 0))
    vx_spec = pl.BlockSpec((1, LANES, lc), lambda b, n: (b, 0, 0))
    sink_spec = pl.BlockSpec((1, 1, ATT_Q_HEADS * ATT_BLOCK), lambda b, n: (layer, 0, 0))
    if local:
        prev = lambda n: jnp.maximum(n * nsub - 1, 0)
        nxt = lambda n: jnp.minimum((n + 1) * nsub, nb - 1)
        kspecs = [pl.BlockSpec((1, ATT_BLOCK, LANES), lambda b, n: (b, prev(n), 0)),
                  pl.BlockSpec((1, tq, LANES), lambda b, n: (b, n, 0)),
                  pl.BlockSpec((1, ATT_BLOCK, LANES), lambda b, n: (b, nxt(n), 0))]
        vspecs = [pl.BlockSpec((1, LANES, ATT_BLOCK), lambda b, n: (b, 0, prev(n))),
                  pl.BlockSpec((1, LANES, tq), lambda b, n: (b, 0, n)),
                  pl.BlockSpec((1, LANES, ATT_BLOCK), lambda b, n: (b, 0, nxt(n)))]
        in_specs = [qspec, *kspecs, kx_spec, *vspecs, vx_spec, sink_spec]
        args = (aq, ak, ak, ak, ak_ctx, avt, avt, avt, avt_ctx, sink)
    else:
        in_specs = [qspec, kx_spec, vx_spec, sink_spec]
        args = (aq, ak_ctx, avt_ctx, sink)
    return pl.pallas_call(
        functools.partial(_attn_kernel, nb=nb, local=local),
        grid=(bsz, nb // nsub),
        in_specs=in_specs,
        out_specs=pl.BlockSpec((1, BRANCH_W, tq), lambda b, n: (b, 0, n)),
        out_shape=jax.ShapeDtypeStruct((bsz, BRANCH_W, seq), BF16),
        compiler_params=_params("parallel", "parallel"),
        name="attention",
    )(*args)


def _merge_kernel(x_ref, mod_ref, g_ref, hf_ref, hb_ref, og_ref, yb_ref, att_ref, yd_ref, wg_ref, wb_ref, wo_ref,
                  o_ref):
    d = D_MODEL
    tm = x_ref.shape[1]
    parts = MERGE_PARTS if tm % (MERGE_PARTS * 2 * LANES) == 0 else 1
    rows = tm // parts
    for part in range(parts):
        tok = slice(part * rows, (part + 1) * rows)
        x = x_ref[0, tok, :]
        u = (_rms(x) * g_ref[0, 0:1, :] * (1.0 + mod_ref[0, 0, 1:2, :]) + mod_ref[0, 0, 0:1, :]).astype(BF16)
        ya_t = (og_ref[0, :, tok].astype(F32)
                * (hf_ref[0, :, tok].astype(F32) + hb_ref[0, :, tok].astype(F32))).astype(BF16)
        ys = (ya_t, yb_ref[0, tok, :], att_ref[0, :, tok], yd_ref[0, tok, :])
        feature_major = (True, False, True, False)
        merged = None
        for n in range(N_BRANCH):
            gate = _sigmoid(_mm(u, wg_ref[0, :, n * d:(n + 1) * d]))
            proj = _tn(ys[n], wb_ref[0, n]) if feature_major[n] else _mm(ys[n], wb_ref[0, n])
            term = gate * proj
            merged = term if merged is None else merged + term
        out = _mm(merged.astype(BF16), wo_ref[0])
        o_ref[0, tok, :] = x + mod_ref[0, 0, 2:3, :] * (_rms(out) * g_ref[0, 1:2, :])


def _merge(x, mod, norm_g, hf, hb, og, yb, att, yd, w_mg, w_branch, w_out, *, layer, mod_row, tm):
    bsz, seq, d = x.shape
    mrow = (lambda b: b) if mod_row is None else (lambda b: mod_row)
    tok = lambda w: pl.BlockSpec((1, tm, w), lambda b, i: (b, i, 0))
    feat = pl.BlockSpec((1, BRANCH_W, tm), lambda b, i: (b, 0, i))
    lsel3 = lambda b, i: (layer, 0, 0)
    return pl.pallas_call(
        _merge_kernel,
        grid=(bsz, seq // tm),
        in_specs=[tok(d),
                  pl.BlockSpec((1, 1, 6, d), lambda b, i: (layer, mrow(b), 0, 0)),
                  _const_spec((1, 4, d), lsel3),
                  feat, feat, feat, tok(BRANCH_W), feat, tok(BRANCH_W),
                  _const_spec((1, d, N_BRANCH * d), lsel3),
                  _const_spec((1, N_BRANCH, BRANCH_W, d), lambda b, i: (layer, 0, 0, 0)),
                  _const_spec((1, d, d), lsel3)],
        out_specs=tok(d),
        out_shape=jax.ShapeDtypeStruct((bsz, seq, d), F32),
        compiler_params=_params("parallel", "parallel"),
        name="merge",
    )(x, mod, norm_g, hf, hb, og, yb, att, yd, w_mg, w_branch, w_out)


def _ffn_kernel(xm_ref, xp_ref, xn_ref, mod_ref, g_ref, wu_ref, wc_ref, wd_ref, o_ref, u_scr, rg_scr, rv_scr,
                h_scr, *, tm, seq):
    u = _modulated_tile(xm_ref, xp_ref, xn_ref, g_ref[0, 2:3, :], mod_ref[0, 0, 3:4, :], mod_ref[0, 0, 4:5, :],
                        tm, seq)
    u_scr[...] = u.astype(BF16)
    cw = FF_CHUNK
    for c in range(D_FF // cw):
        gcols = slice(c * cw, (c + 1) * cw)
        vcols = slice(D_FF + c * cw, D_FF + (c + 1) * cw)
        rg_scr[...] = _mm(u_scr[...], wu_ref[0, :, gcols])
        rv_scr[...] = _mm(u_scr[...], wu_ref[0, :, vcols])
        half = _conv3_taps(rg_scr, 0.5 * wc_ref[0, :, gcols], tm)
        val = _conv3(rv_scr, wc_ref, vcols, tm)
        h_scr[:, gcols] = ((half + half * jnp.tanh(half)) * val).astype(BF16)
    out = _mm(h_scr[...], wd_ref[0])
    x = xm_ref[0]
    o_ref[0] = x + mod_ref[0, 0, 5:6, :] * (_rms(out) * g_ref[0, 3:4, :])


def _ffn(x, mod, norm_g, w_up, w_conv, w_down, *, layer, mod_row, tm):
    bsz, seq, d = x.shape
    main, prev, nxt = _halo_specs(tm, d, seq)
    mrow = (lambda b: b) if mod_row is None else (lambda b: mod_row)
    lsel3 = lambda b, i: (layer, 0, 0)
    return pl.pallas_call(
        functools.partial(_ffn_kernel, tm=tm, seq=seq),
        grid=(bsz, seq // tm),
        in_specs=[main, prev, nxt,
                  pl.BlockSpec((1, 1, 6, d), lambda b, i: (layer, mrow(b), 0, 0)),
                  _const_spec((1, 4, d), lsel3),
                  _const_spec((1, d, 2 * D_FF), lsel3),
                  _const_spec((1, 3, 2 * D_FF), lsel3),
                  _const_spec((1, D_FF, d), lsel3)],
        out_specs=main,
        out_shape=jax.ShapeDtypeStruct((bsz, seq, d), F32),
        scratch_shapes=[pltpu.VMEM((tm + 2 * HALO, d), BF16), pltpu.VMEM((tm + 2 * HALO, FF_CHUNK), F32),
                        pltpu.VMEM((tm + 2 * HALO, FF_CHUNK), F32), pltpu.VMEM((tm, D_FF), BF16)],
        compiler_params=_params("parallel", "parallel"),
        name="convffn",
    )(x, x, x, mod, norm_g, w_up, w_conv, w_down)


def _rope_tables(seq):
    t = jnp.arange(seq, dtype=jnp.int32)[:, None]
    lane = jnp.arange(LANES, dtype=jnp.int32)[None, :]
    in_head = lane % ATT_HEAD_DIM
    n_freq = ATT_HEAD_DIM // 4
    inv_freq = ROPE_BASE ** (-(in_head % n_freq).astype(F32) / n_freq)
    pos = jnp.where(in_head < ATT_HEAD_DIM // 2, t // GRID_W, t % GRID_W).astype(F32)
    ang = pos * inv_freq
    sign = jnp.where((in_head % (2 * n_freq)) < n_freq, -1.0, 1.0)
    return jnp.cos(ang), sign * jnp.sin(ang)


def _tile_rows(seq):
    return min(seq, 512)


def kernel(x, c, ctx, c_ctx, w_mod, b_mod, norm_g, w_in, mlstm_qk_conv, mlstm_gate_bias, pool_w, pool_scale,
           attn_sink, sconv_w, w_branch, w_out, ffn_up, ffn_conv, ffn_down):
    bsz, seq, d = x.shape
    lc = ctx.shape[1]
    depth = w_mod.shape[0]
    assert d == D_MODEL and seq % 512 == 0 and lc % 256 == 0 and bsz + 1 <= 8

    cond = jnp.concatenate([c, c_ctx[None, :], jnp.zeros((8 - bsz - 1, d), F32)], axis=0)
    mod = _modulation(cond, w_mod, b_mod).reshape(depth, 8, 6, d)

    g0 = BRANCH_W * 4
    g1 = g0 + 4 * MLSTM_HEADS
    mg0 = w_in.shape[-1] - N_BRANCH * d
    p0 = g1
    aq0 = p0 + BRANCH_W
    ak0 = aq0 + BRANCH_W
    av0 = ak0 + ATT_KV_HEADS * ATT_HEAD_DIM
    sc0 = av0 + ATT_KV_HEADS * ATT_HEAD_DIM
    w_aq = w_in[:, :, aq0:ak0].reshape(depth, d, ATT_KV_HEADS, ATT_Q_HEADS // ATT_KV_HEADS, ATT_HEAD_DIM)
    w_aq = jnp.swapaxes(w_aq, 2, 3).reshape(depth, d, BRANCH_W)
    v0 = 2 * BRANCH_W
    w1 = jnp.concatenate([w_in[:, :, :v0], w_in[:, :, p0:aq0], w_aq, w_in[:, :, ak0:av0], w_in[:, :, sc0:mg0]],
                         axis=-1).astype(BF16)
    perm = jnp.array(_GATE_PERM)
    wt = jnp.concatenate([w_in[:, :, v0:g0], w_in[:, :, av0:sc0], w_in[:, :, g0:g1][:, :, perm]], axis=-1)
    wt = jnp.swapaxes(wt, 1, 2).astype(BF16)
    gb = mlstm_gate_bias.reshape(depth, 4 * MLSTM_HEADS)[:, perm][:, :, None]
    w_mg = w_in[:, :, mg0:].astype(BF16)
    pw = pool_w.astype(BF16)
    ps = pool_scale[:, None, :]
    sink_heads = attn_sink.reshape(depth, ATT_KV_HEADS, ATT_Q_HEADS // ATT_KV_HEADS).swapaxes(1, 2)
    sink = jnp.broadcast_to(sink_heads.reshape(depth, 1, ATT_Q_HEADS, 1),
                            (depth, 1, ATT_Q_HEADS, ATT_BLOCK)).reshape(depth, 1, ATT_Q_HEADS * ATT_BLOCK)
    wb = w_branch.astype(BF16)
    wo = w_out.astype(BF16)
    wu = ffn_up.astype(BF16)
    wc = ffn_conv
    wd = ffn_down.astype(BF16)

    cos_x, sin_x = _rope_tables(seq)
    cos_c, sin_c = jnp.ones((lc, LANES), F32), jnp.zeros((lc, LANES), F32)

    hd = MLSTM_HD
    zero_c = jnp.zeros((bsz, MLSTM_HEADS, 2 * hd, hd), F32)
    zero_m = jnp.zeros((bsz, 8, LANES), F32)
    tm_x, tm_c = _tile_rows(seq), _tile_rows(lc)
    scan = functools.partial(_mlstm, chunk=MLSTM_CHUNK)

    h = ctx
    for l in range(depth):
        ctx_out = l < depth - 1
        proj = functools.partial(_inproj, mod=mod, norm_g=norm_g, w1=w1, wt=wt, gb=gb, qkc=mlstm_qk_conv,
                                 scw=sconv_w, pw=pw, ps=ps, layer=l)
        qk_c, v_c, og_c, gt_c, yb_c, aq_c, ak_c, avt_c, yd_c = proj(h, cos=cos_c, sin=sin_c, mod_row=bsz, tm=tm_c)
        qk_x, v_x, og_x, gt_x, yb_x, aq_x, ak_x, avt_x, yd_x = proj(x, cos=cos_x, sin=sin_x, mod_row=None, tm=tm_x)
        hf_c, cf, mf = scan(qk_c, v_c, gt_c, zero_c, zero_m, reverse=False)
        hb_c, cb, mb = scan(qk_c, v_c, gt_c, zero_c, zero_m, reverse=True)
        hf_x, _, _ = scan(qk_x, v_x, gt_x, cf, mf, reverse=False)
        hb_x, _, _ = scan(qk_x, v_x, gt_x, cb, mb, reverse=True)
        att_x = _attention(aq_x, ak_x, avt_x, ak_c, avt_c, sink, layer=l, local=True)
        tail = dict(norm_g=norm_g, layer=l)
        x = _merge(x, mod, hf=hf_x, hb=hb_x, og=og_x, yb=yb_x, att=att_x, yd=yd_x, w_mg=w_mg, w_branch=wb,
                   w_out=wo, mod_row=None, tm=tm_x, **tail)
        x = _ffn(x, mod, w_up=wu, w_conv=wc, w_down=wd, mod_row=None, tm=tm_x, **tail)
        if ctx_out:
            att_c = _attention(aq_c, ak_c, avt_c, ak_c, avt_c, sink, layer=l, local=False)
            h = _merge(h, mod, hf=hf_c, hb=hb_c, og=og_c, yb=yb_c, att=att_c, yd=yd_c, w_mg=w_mg, w_branch=wb,
                       w_out=wo, mod_row=bsz, tm=tm_c, **tail)
            h = _ffn(h, mod, w_up=wu, w_conv=wc, w_down=wd, mod_row=bsz, tm=tm_c, **tail)
    return x
```

```python
import functools

import jax
import jax.numpy as jnp
from jax import lax
from jax.experimental import pallas as pl
from jax.experimental.pallas import tpu as pltpu

F32 = jnp.float32
BF16 = jnp.bfloat16
HIGHEST = lax.Precision.HIGHEST

D_MODEL = 1024
GRID_W = 64
N_BRANCH = 4
BRANCH_W = D_MODEL // 2
MLSTM_HEADS = 4
MLSTM_HD = BRANCH_W // MLSTM_HEADS
MLSTM_CHUNK = 256
MLSTM_CHUNKS_PER_STEP = 8
POOL_HALF = (1, 2, 4, 8)
POOL_GW = BRANCH_W // len(POOL_HALF)
ATT_HEAD_DIM = 64
ATT_Q_HEADS = BRANCH_W // ATT_HEAD_DIM
ATT_KV_HEADS = 2
ATT_BLOCK = 128
ATT_SUB = 8
LOG2E = 1.4426950408889634
ATT_Q_SCALE = ATT_HEAD_DIM ** -0.5 * LOG2E
ROPE_BASE = 10000.0
D_FF = 2816
EPS = 1e-6

LANES = 128
HALO = 8
FF_CHUNK = 256
MERGE_PARTS = 2
VMEM_LIMIT = 56 * 1024 * 1024

_OFF = {}
_acc = 0
for _name, _w in (("q", 512), ("k", 512), ("pool", 512), ("aq", 512), ("ak", 128), ("sb", 512), ("sc", 512),
                  ("sx", 512)):
    _OFF[_name] = _acc
    _acc += _w
W1_COLS = _acc
_TOFF = {}
_acc = 0
for _name, _w in (("v", 512), ("o", 512), ("av", 128), ("gates", 16)):
    _TOFF[_name] = _acc
    _acc += _w
WT_ROWS = _acc
_GATE_PERM = (0, 1, 2, 3, 8, 9, 10, 11, 4, 5, 6, 7, 12, 13, 14, 15)


def _iota(shape, dim):
    return lax.broadcasted_iota(jnp.int32, shape, dim)


def _rms(x):
    return x * lax.rsqrt(jnp.mean(x * x, axis=-1, keepdims=True) + EPS)


def _sigmoid(x):
    return 1.0 / (1.0 + jnp.exp(-x))


def _nt(a, b, **kw):
    return lax.dot_general(a, b, (((1,), (1,)), ((), ())), preferred_element_type=F32, **kw)


def _tn(a, b, **kw):
    return lax.dot_general(a, b, (((0,), (0,)), ((), ())), preferred_element_type=F32, **kw)


def _mm(a, b, **kw):
    return jnp.dot(a, b, preferred_element_type=F32, **kw)


def _params(*sem):
    return pltpu.CompilerParams(dimension_semantics=sem, vmem_limit_bytes=VMEM_LIMIT)


def _stream_spec(shape, index_map):
    return pl.BlockSpec(shape, index_map, pipeline_mode=pl.Buffered(3))


def _const_spec(shape, index_map):
    return pl.BlockSpec(shape, index_map, pipeline_mode=pl.Buffered(1))


def _mod_kernel(c_ref, w_ref, b_ref, o_ref):
    cond = c_ref[...]
    sc = cond * _sigmoid(cond)
    o_ref[0] = _mm(sc, w_ref[0], precision=HIGHEST) + b_ref[0]


def _modulation(cond, w_mod, b_mod):
    depth, d, n = w_mod.shape
    rows = cond.shape[0]
    tn = d
    return pl.pallas_call(
        _mod_kernel,
        grid=(depth, n // tn),
        in_specs=[pl.BlockSpec((rows, d), lambda l, j: (0, 0)),
                  pl.BlockSpec((1, d, tn), lambda l, j: (l, 0, j)),
                  pl.BlockSpec((1, 1, tn), lambda l, j: (l, 0, j))],
        out_specs=pl.BlockSpec((1, rows, tn), lambda l, j: (l, 0, j)),
        out_shape=jax.ShapeDtypeStruct((depth, rows, n), F32),
        compiler_params=_params("parallel", "parallel"),
        name="modulation",
    )(cond, w_mod, b_mod.reshape(depth, 1, n))


def _halo_specs(tm, d, seq):
    nh = seq // HALO
    per = tm // HALO
    main = pl.BlockSpec((1, tm, d), lambda b, i: (b, i, 0))
    prev = pl.BlockSpec((1, HALO, d), lambda b, i: (b, jnp.maximum(i * per - 1, 0), 0))
    nxt = pl.BlockSpec((1, HALO, d), lambda b, i: (b, jnp.minimum((i + 1) * per, nh - 1), 0))
    return main, prev, nxt


def _modulated_tile(xm_ref, xp_ref, xn_ref, gain, shift, scale, tm, seq):
    i = pl.program_id(1)
    mult = gain * (1.0 + scale)
    mod = lambda x: _rms(x) * mult + shift
    before = jnp.where(i > 0, mod(xp_ref[0]), 0.0)
    after = jnp.where((i + 1) * tm < seq, mod(xn_ref[0]), 0.0)
    return jnp.concatenate([before, mod(xm_ref[0]), after], axis=0)


def _conv3_taps(r_ref, w, tm):
    return (r_ref[HALO - 1:HALO - 1 + tm, :] * w[0:1] + r_ref[HALO:HALO + tm, :] * w[1:2]
            + r_ref[HALO + 1:HALO + 1 + tm, :] * w[2:3])


def _conv3(r_ref, w_ref, cols, tm):
    return _conv3_taps(r_ref, w_ref[0, :, cols], tm)


def _rope(r, cos, sin):
    lane = _iota((1, LANES), 1)
    first = (lane % 32) < 16
    swapped = jnp.where(first, pltpu.roll(r, LANES - 16, 1), pltpu.roll(r, 16, 1))
    return r * cos + swapped * sin


def _inproj_kernel(xm_ref, xp_ref, xn_ref, mod_ref, g_ref, w_ref, wt_ref, gb_ref, qkc_ref, scw_ref, pw_ref,
                   ps_ref, cos_ref, sin_ref,
                   qk_out, vt_out, ogt_out, gates_out, yb_out, aq_out, ak_out, avt_out, yd_out,
                   u_scr, um_scr, r_scr, r2_scr, *, tm, seq):
    i = pl.program_id(1)
    u = _modulated_tile(xm_ref, xp_ref, xn_ref, g_ref[0, 0:1, :], mod_ref[0, 0, 0:1, :], mod_ref[0, 0, 1:2, :],
                        tm, seq)
    u_scr[...] = u.astype(BF16)
    um_scr[...] = u[HALO:HALO + tm].astype(BF16)

    def wcols(name, width):
        o = _OFF[name]
        return w_ref[0, :, o:o + width]

    for c, name in enumerate(("q", "k")):
        cols = slice(c * BRANCH_W, (c + 1) * BRANCH_W)
        scr = r_scr if name == "q" else r2_scr
        scr[...] = _mm(u_scr[...], wcols(name, BRANCH_W))
        if name == "k":
            r_scr[...] = _mm(u_scr[...], wcols("pool", BRANCH_W))
        half = _conv3_taps(scr, 0.5 * qkc_ref[0, :, cols], tm)
        a = half + half * jnp.tanh(half)
        if name == "k":
            a = a * MLSTM_HD ** -0.5
        qk_out[0, :, cols] = a.astype(BF16)

    um = um_scr[...]

    def trows(name, height):
        o = _TOFF[name]
        return wt_ref[0, o:o + height, :]

    g = _nt(trows("gates", 16), um) + gb_ref[0]
    logsig = jnp.minimum(g, 0.0) - jnp.log(1.0 + jnp.exp(-jnp.abs(g)))
    is_forget = (_iota((16, 1), 0) % 8) >= MLSTM_HEADS
    g = jnp.where(is_forget, logsig, g) * LOG2E
    gates_out[0, 0] = g[0:8]
    gates_out[0, 1] = g[8:16]

    assert max(POOL_HALF) <= HALO
    t_top = i * tm + _iota((HALO, 1), 0)
    t_bot = t_top + (tm - HALO)
    for gi, half in enumerate(POOL_HALF):
        cols = slice(gi * POOL_GW, (gi + 1) * POOL_GW)
        acc = r_scr[HALO - half:HALO - half + tm, cols]
        for o in range(-half + 1, half):
            acc = acc + r_scr[HALO + o:HALO + o + tm, cols]
        inv = lambda t: 1.0 / (jnp.minimum(t + half, seq) - jnp.maximum(t - half, 0)).astype(F32)
        mean = jnp.concatenate([acc[:HALO] * inv(t_top), acc[HALO:tm - HALO] * (0.5 / half),
                                acc[tm - HALO:] * inv(t_bot)], axis=0)
        delta = mean - r_scr[HALO:HALO + tm, cols]
        y = _mm(delta.astype(BF16), pw_ref[0, gi]) * ps_ref[0, 0:1, cols]
        yb_out[0, :, cols] = y.astype(BF16)
        if gi == 0:
            rb = _mm(um, wcols("sb", BRANCH_W))
        elif gi == 1:
            rc = _mm(u_scr[...], wcols("sc", BRANCH_W))
        elif gi == 2:
            r2_scr[...] = rc * _mm(u_scr[...], wcols("sx", BRANCH_W))
        else:
            rq = _mm(um, wcols("aq", BRANCH_W))

    cos = cos_ref[...]
    sin = sin_ref[...]
    for j in range(BRANCH_W // LANES):
        cols = slice(j * LANES, (j + 1) * LANES)
        aq_out[0, :, cols] = (_rope(rq[:, cols], cos, sin) * ATT_Q_SCALE).astype(BF16)
        if j == 0:
            vt_out[0] = _nt(trows("v", BRANCH_W), um).astype(BF16)
        elif j == 2:
            ogt_out[0] = _sigmoid(_nt(trows("o", BRANCH_W), um)).astype(BF16)
    avt_out[0] = _nt(trows("av", LANES), um).astype(BF16)
    ak_out[0] = _rope(_mm(um, wcols("ak", LANES)), cos, sin).astype(BF16)

    yd_out[0] = (rb * _conv3(r2_scr, scw_ref, slice(0, BRANCH_W), tm)).astype(BF16)


def _inproj(x, mod, norm_g, w1, wt, gb, qkc, scw, pw, ps, cos, sin, *, layer, mod_row, tm):
    bsz, seq, d = x.shape
    nt = seq // tm
    main, prev, nxt = _halo_specs(tm, d, seq)
    mrow = (lambda b: b) if mod_row is None else (lambda b: mod_row)
    lsel3 = lambda b, i: (layer, 0, 0)
    in_specs = [
        main, prev, nxt,
        pl.BlockSpec((1, 1, 6, d), lambda b, i: (layer, mrow(b), 0, 0)),
        _const_spec((1, 4, d), lsel3),
        _const_spec((1, d, W1_COLS), lsel3),
        _const_spec((1, WT_ROWS, d), lsel3),
        _const_spec((1, 16, 1), lsel3),
        _const_spec((1, 3, 2 * BRANCH_W), lsel3),
        _const_spec((1, 3, BRANCH_W), lsel3),
        _const_spec((1, len(POOL_HALF), POOL_GW, POOL_GW), lambda b, i: (layer, 0, 0, 0)),
        _const_spec((1, 1, BRANCH_W), lsel3),
        pl.BlockSpec((tm, LANES), lambda b, i: (i, 0)),
        pl.BlockSpec((tm, LANES), lambda b, i: (i, 0)),
    ]
    tok = lambda w: pl.BlockSpec((1, tm, w), lambda b, i: (b, i, 0))
    feat = lambda w: pl.BlockSpec((1, w, tm), lambda b, i: (b, 0, i))
    out_specs = [tok(2 * BRANCH_W), feat(BRANCH_W), feat(BRANCH_W),
                 pl.BlockSpec((1, 2, 8, tm), lambda b, i: (b, 0, 0, i)),
                 tok(BRANCH_W), tok(BRANCH_W), tok(LANES), feat(LANES), tok(BRANCH_W)]
    act = lambda w: jax.ShapeDtypeStruct((bsz, seq, w), BF16)
    act_t = lambda w: jax.ShapeDtypeStruct((bsz, w, seq), BF16)
    out_shape = [act(2 * BRANCH_W), act_t(BRANCH_W), act_t(BRANCH_W),
                 jax.ShapeDtypeStruct((bsz, 2, 8, seq), F32),
                 act(BRANCH_W), act(BRANCH_W), act(LANES), act_t(LANES), act(BRANCH_W)]
    return pl.pallas_call(
        functools.partial(_inproj_kernel, tm=tm, seq=seq),
        grid=(bsz, nt),
        in_specs=in_specs,
        out_specs=out_specs,
        out_shape=out_shape,
        scratch_shapes=[pltpu.VMEM((tm + 2 * HALO, d), BF16), pltpu.VMEM((tm, d), BF16),
                        pltpu.VMEM((tm + 2 * HALO, BRANCH_W), F32), pltpu.VMEM((tm + 2 * HALO, BRANCH_W), F32)],
        compiler_params=_params("parallel", "parallel"),
        name="inproj",
    )(x, x, x, mod, norm_g, w1, wt, gb, qkc, scw, pw, ps, cos, sin)


def _split3(x):
    x = jnp.concatenate([x, jnp.zeros_like(x)], axis=0)
    hi = x.astype(BF16)
    r1 = x - hi.astype(F32)
    mid = r1.astype(BF16)
    lo = (r1 - mid.astype(F32)).astype(BF16)
    return jnp.concatenate([hi, mid, lo], axis=0)


def _sum3(y):
    return (y[0:8] + y[16:24]) + y[32:40]


def _mlstm_kernel(qk_ref, vt_ref, g_ref, c0_ref, m0_ref, h_out, c_out, m_out, *, chunk, reverse):
    L = chunk
    hd = MLSTM_HD
    nck = qk_ref.shape[1] // L

    @pl.when(pl.program_id(1) == 0)
    def _():
        c_out[...] = c0_ref[...]
        m_out[...] = m0_ref[...]

    s_idx = _iota((L, L), 0)
    t_idx = _iota((L, L), 1)
    seen = (s_idx >= t_idx) if reverse else (s_idx <= t_idx)
    seen_bf = seen.astype(BF16)
    is_last = _iota((1, L), 1) == (0 if reverse else L - 1)
    ones = jnp.ones((hd, L), BF16)
    neg_inf = -jnp.inf

    heads = range(MLSTM_HEADS)
    wide = lambda parts: jnp.concatenate(parts, axis=1)
    toks = [slice(((nck - 1 - i) if reverse else i) * L, (((nck - 1 - i) if reverse else i) + 1) * L)
            for i in range(nck)]

    qs, ks, vaugs, b_rows, ig_rows, keys, scores = [], [], [], [], [], [], []
    for tok in toks:
        g = g_ref[0, 0, :, tok]
        g3 = _split3(g)
        cum_row = _sum3(_mm(g3, seen_bf))
        key_rows = cum_row[MLSTM_HEADS:] - g[:MLSTM_HEADS]
        key_cols = jnp.transpose(jnp.concatenate([key_rows, jnp.zeros_like(key_rows)], axis=0))
        for h in heads:
            qs.append(qk_ref[0, tok, h * hd:(h + 1) * hd])
            ks.append(qk_ref[0, tok, BRANCH_W + h * hd:BRANCH_W + (h + 1) * hd])
            vaugs.append(jnp.concatenate([vt_ref[0, h * hd:(h + 1) * hd, tok], ones], axis=0))
            b_rows.append(cum_row[MLSTM_HEADS + h:MLSTM_HEADS + h + 1, :])
            ig_rows.append(g[h:h + 1, :])
            keys.append(jnp.broadcast_to(key_cols[:, h:h + 1], (L, L)))
            scores.append(_nt(ks[-1], qs[-1]))
    n_pairs = len(qs)
    b_row = wide(b_rows)
    s_w = _iota((L, n_pairs * L), 0)
    t_w = _iota((L, n_pairs * L), 1) % L
    seen_w = (s_w >= t_w) if reverse else (s_w <= t_w)
    dmat = jnp.where(seen_w, b_row - wide(keys), neg_inf)
    m_intra = jnp.max(dmat, axis=0, keepdims=True)
    p = (wide(scores) * jnp.exp2(dmat - m_intra)).astype(BF16)
    b_tots = [jnp.max(jnp.where(is_last, b, neg_inf), axis=1, keepdims=True) for b in b_rows]
    u_loc = wide([bt - b + ig for bt, b, ig in zip(b_tots, b_rows, ig_rows)])
    m_locs = [jnp.max(u_loc[:, e * L:(e + 1) * L], axis=1, keepdims=True) for e in range(n_pairs)]
    wk_loc = jnp.exp2(u_loc - wide([jnp.broadcast_to(m, (1, L)) for m in m_locs]))

    def chunk_products(i):
        pairs = range(i * MLSTM_HEADS, (i + 1) * MLSTM_HEADS)
        r = {e: _mm(vaugs[e], p[:, e * L:(e + 1) * L]) for e in pairs}
        kv = {e: _mm((vaugs[e].astype(F32) * wk_loc[:, e * L:(e + 1) * L]).astype(BF16), ks[e])
              for e in pairs}
        return r, kv

    ready = chunk_products(0)
    for i, tok in enumerate(toks):
        r_intra, kv_loc = ready
        if i + 1 < nck:
            ready = chunk_products(i + 1)
        for h in heads:
            e = i * MLSTM_HEADS + h
            lanes = slice(e * L, (e + 1) * L)
            m_prev = m_out[0, h:h + 1, 0:1]
            c_prev = c_out[0, h]
            a = b_rows[e] + m_prev
            mt = jnp.maximum(a, m_intra[:, lanes])
            r = (jnp.exp2(m_intra[:, lanes] - mt) * r_intra[e]
                 + jnp.exp2(a - mt) * _nt(c_prev.astype(BF16), qs[e]))
            den = jnp.maximum(jnp.abs(r[hd:]), jnp.exp2(-mt))
            h_out[0, h * hd:(h + 1) * hd, tok] = (r[:hd] / den).astype(BF16)
            m_new = jnp.max(jnp.where(is_last, mt, neg_inf), axis=1, keepdims=True)
            c_out[0, h] = (jnp.exp2(b_tots[e] + m_prev - m_new) * c_prev
                           + jnp.exp2(m_locs[e] - m_new) * kv_loc[e])
            m_out[0, h:h + 1, :] = jnp.broadcast_to(m_new, (1, LANES))


def _mlstm(qk, vt, gates, c0, m0, *, chunk, reverse):
    bsz, seq, _ = qk.shape
    nck = min(MLSTM_CHUNKS_PER_STEP, seq // chunk)
    tb = nck * chunk
    ns = seq // tb
    hd = MLSTM_HD
    direction = 1 if reverse else 0
    blk = (lambda j: ns - 1 - j) if reverse else (lambda j: j)
    state_c = pl.BlockSpec((1, MLSTM_HEADS, 2 * hd, hd), lambda b, j: (b, 0, 0, 0))
    state_m = pl.BlockSpec((1, 8, LANES), lambda b, j: (b, 0, 0))
    return pl.pallas_call(
        functools.partial(_mlstm_kernel, chunk=chunk, reverse=reverse),
        grid=(bsz, ns),
        in_specs=[pl.BlockSpec((1, tb, 2 * BRANCH_W), lambda b, j: (b, blk(j), 0)),
                  pl.BlockSpec((1, BRANCH_W, tb), lambda b, j: (b, 0, blk(j))),
                  pl.BlockSpec((1, 1, 8, tb), lambda b, j: (b, direction, 0, blk(j))),
                  state_c, state_m],
        out_specs=[pl.BlockSpec((1, BRANCH_W, tb), lambda b, j: (b, 0, blk(j))),
                   state_c, state_m],
        out_shape=[jax.ShapeDtypeStruct((bsz, BRANCH_W, seq), BF16),
                   jax.ShapeDtypeStruct((bsz, MLSTM_HEADS, 2 * hd, hd), F32),
                   jax.ShapeDtypeStruct((bsz, 8, LANES), F32)],
        compiler_params=_params("parallel", "arbitrary"),
        name="mlstm_bwd" if reverse else "mlstm_fwd",
    )(qk, vt, gates, c0, m0)


def _attn_kernel(*refs, nb, local):
    blk = ATT_BLOCK
    hd = ATT_HEAD_DIM
    if local:
        q_ref, kp_ref, kc_ref, kn_ref, kx_ref, vp_ref, vc_ref, vn_ref, vx_ref, sink_ref, edge_ref, o_ref = refs
        has_prev = edge_ref[0, :, 0:1] > 0
        has_next = edge_ref[0, :, 1:2] > 0
        k_loc = jnp.concatenate([kp_ref[0], kc_ref[0], kn_ref[0]], axis=0)
        vt_loc = jnp.concatenate([vp_ref[0], vc_ref[0], vn_ref[0]], axis=1)
    else:
        q_ref, kx_ref, vx_ref, sink_ref, o_ref = refs
    nsub = q_ref.shape[1] // blk
    lo = _iota((1, LANES), 1) < hd
    nq = ATT_Q_HEADS * blk
    sk = sink_ref[0] * LOG2E
    if local:
        kj = _iota((blk, nq), 0)
        qi = _iota((blk, nq), 1) % blk
        after = kj >= qi
        before = kj <= qi
        band_prev = jnp.where(after, 0.0, -jnp.inf)
        band_next = jnp.where(before, 0.0, -jnp.inf)

    scores, values, prev_bias, next_bias = [], [], [], []
    for i in range(nsub):
        if local:
            k = jnp.concatenate([k_loc[i * blk:(i + 3) * blk], kx_ref[0]], axis=0)
            vt = jnp.concatenate([vt_loc[:, i * blk:(i + 3) * blk], vx_ref[0]], axis=1)
            prev_bias.append(band_prev if i > 0 else jnp.where(after & has_prev, 0.0, -jnp.inf))
            next_bias.append(band_next if i < nsub - 1 else jnp.where(before & has_next, 0.0, -jnp.inf))
        else:
            k = kx_ref[0]
            vt = vx_ref[0]
        nk = k.shape[0]
        values.append(jnp.concatenate([vt, jnp.ones((16, nk), BF16)], axis=0))
        parts = []
        for j in range(BRANCH_W // LANES):
            q2 = q_ref[0, i * blk:(i + 1) * blk, j * LANES:(j + 1) * LANES]
            zero = jnp.zeros_like(q2)
            parts += [jnp.where(lo, q2, zero), jnp.where(lo, zero, q2)]
        scores.append(_nt(k, jnp.concatenate(parts, axis=0)))
    s = jnp.concatenate(scores, axis=1)
    if local:
        s = jnp.concatenate([s[0:blk] + jnp.concatenate(prev_bias, axis=1), s[blk:2 * blk],
                             s[2 * blk:3 * blk] + jnp.concatenate(next_bias, axis=1), s[3 * blk:]], axis=0)
    sk = jnp.concatenate([sk] * nsub, axis=1)
    m = jnp.maximum(jnp.max(s, axis=0, keepdims=True), sk)
    e = jnp.exp2(s - m).astype(BF16)
    extra = jnp.exp2(sk - m)
    for i in range(nsub):
        r = _mm(values[i], e[:, i * nq:(i + 1) * nq])
        den = r[2 * hd:2 * hd + 1, :] + extra[:, i * nq:(i + 1) * nq]
        for p in range(ATT_Q_HEADS):
            c = p % 2
            head = p // 2 + 4 * c
            cols = slice(p * blk, (p + 1) * blk)
            o_ref[0, head * hd:(head + 1) * hd, i * blk:(i + 1) * blk] = (
                r[c * hd:(c + 1) * hd, cols] / den[:, cols]).astype(BF16)


def _attention(aq, ak, avt, ak_ctx, avt_ctx, sink, *, layer, local):
    bsz, seq, _ = aq.shape
    nb = seq // ATT_BLOCK
    nsub = min(ATT_SUB, nb)
    tq = nsub * ATT_BLOCK
    lc = ak_ctx.shape[1]
    qspec = (_stream_spec if local else pl.BlockSpec)((1, tq, BRANCH_W), lambda b, n: (b, n, 0))
    kx_spec = pl.BlockSpec((1, lc, LANES), lambda b, n: (b, 0, 0))
    vx_spec = pl.BlockSpec((1, LANES, lc), lambda b, n: (b, 0, 0))
    sink_spec = pl.BlockSpec((1, 1, ATT_Q_HEADS * ATT_BLOCK), lambda b, n: (layer, 0, 0))
    if local:
        prev = lambda n: jnp.maximum(n * nsub - 1, 0)
        nxt = lambda n: jnp.minimum((n + 1) * nsub, nb - 1)
        kspecs = [pl.BlockSpec((1, ATT_BLOCK, LANES), lambda b, n: (b, prev(n), 0)),
                  pl.BlockSpec((1, tq, LANES), lambda b, n: (b, n, 0)),
                  pl.BlockSpec((1, ATT_BLOCK, LANES), lambda b, n: (b, nxt(n), 0))]
        vspecs = [pl.BlockSpec((1, LANES, ATT_BLOCK), lambda b, n: (b, 0, prev(n))),
                  pl.BlockSpec((1, LANES, tq), lambda b, n: (b, 0, n)),
                  pl.BlockSpec((1, LANES, ATT_BLOCK), lambda b, n: (b, 0, nxt(n)))]
        nsteps = nb // nsub
        pos = jnp.arange(nsteps, dtype=jnp.int32)[:, None, None]
        lane = jnp.arange(LANES, dtype=jnp.int32)[None, None, :]
        edge = jnp.where(lane == 0, pos > 0, (lane == 1) & (pos < nsteps - 1)).astype(jnp.int32)
        edge_spec = pl.BlockSpec((1, 1, LANES), lambda b, n: (n, 0, 0))
        in_specs = [qspec, *kspecs, kx_spec, *vspecs, vx_spec, sink_spec, edge_spec]
        args = (aq, ak, ak, ak, ak_ctx, avt, avt, avt, avt_ctx, sink, edge)
    else:
        in_specs = [qspec, kx_spec, vx_spec, sink_spec]
        args = (aq, ak_ctx, avt_ctx, sink)
    out_spec = pl.BlockSpec((1, BRANCH_W, tq), lambda b, n: (b, 0, n))
    out_shape = jax.ShapeDtypeStruct((bsz, BRANCH_W, seq), BF16)
    grid = (bsz, nb // nsub)
    if not local:
        return pl.pallas_call(
            functools.partial(_attn_kernel, nb=nb, local=local),
            grid=grid,
            in_specs=in_specs,
            out_specs=out_spec,
            out_shape=out_shape,
            compiler_params=_params("parallel", "parallel"),
            name="attention",
        )(*args)

    def outer(*refs):
        pltpu.emit_pipeline(functools.partial(_attn_kernel, nb=nb, local=True), grid=grid, in_specs=in_specs,
                            out_specs=[out_spec])(*refs)

    return pl.pallas_call(
        outer,
        in_specs=[pl.BlockSpec(memory_space=pl.ANY)] * len(args),
        out_specs=pl.BlockSpec(memory_space=pl.ANY),
        out_shape=out_shape,
        compiler_params=pltpu.CompilerParams(vmem_limit_bytes=VMEM_LIMIT),
        name="attention",
    )(*args)


def _merge_kernel(x_ref, mod_ref, g_ref, hf_ref, hb_ref, og_ref, yb_ref, att_ref, yd_ref, wg_ref, wb_ref, wo_ref,
                  o_ref):
    d = D_MODEL
    tm = x_ref.shape[1]
    parts = MERGE_PARTS if tm % (MERGE_PARTS * 2 * LANES) == 0 else 1
    rows = tm // parts
    for part in range(parts):
        tok = slice(part * rows, (part + 1) * rows)
        x = x_ref[0, tok, :]
        u = (_rms(x) * g_ref[0, 0:1, :] * (1.0 + mod_ref[0, 0, 1:2, :]) + mod_ref[0, 0, 0:1, :]).astype(BF16)
        ya_t = (og_ref[0, :, tok].astype(F32)
                * (hf_ref[0, :, tok].astype(F32) + hb_ref[0, :, tok].astype(F32))).astype(BF16)
        ys = (ya_t, yb_ref[0, tok, :], att_ref[0, :, tok], yd_ref[0, tok, :])
        feature_major = (True, False, True, False)
        merged = None
        for n in range(N_BRANCH):
            gate = _sigmoid(_mm(u, wg_ref[0, :, n * d:(n + 1) * d]))
            proj = _tn(ys[n], wb_ref[0, n]) if feature_major[n] else _mm(ys[n], wb_ref[0, n])
            term = gate * proj
            merged = term if merged is None else merged + term
        out = _mm(merged.astype(BF16), wo_ref[0])
        o_ref[0, tok, :] = x + mod_ref[0, 0, 2:3, :] * (_rms(out) * g_ref[0, 1:2, :])


def _merge(x, mod, norm_g, hf, hb, og, yb, att, yd, w_mg, w_branch, w_out, *, layer, mod_row, tm):
    bsz, seq, d = x.shape
    mrow = (lambda b: b) if mod_row is None else (lambda b: mod_row)
    tok = lambda w: pl.BlockSpec((1, tm, w), lambda b, i: (b, i, 0))
    feat = pl.BlockSpec((1, BRANCH_W, tm), lambda b, i: (b, 0, i))
    lsel3 = lambda b, i: (layer, 0, 0)
    return pl.pallas_call(
        _merge_kernel,
        grid=(bsz, seq // tm),
        in_specs=[tok(d),
                  pl.BlockSpec((1, 1, 6, d), lambda b, i: (layer, mrow(b), 0, 0)),
                  _const_spec((1, 4, d), lsel3),
                  feat, feat, feat, tok(BRANCH_W), feat, tok(BRANCH_W),
                  _const_spec((1, d, N_BRANCH * d), lsel3),
                  _const_spec((1, N_BRANCH, BRANCH_W, d), lambda b, i: (layer, 0, 0, 0)),
                  _const_spec((1, d, d), lsel3)],
        out_specs=tok(d),
        out_shape=jax.ShapeDtypeStruct((bsz, seq, d), F32),
        compiler_params=_params("parallel", "parallel"),
        name="merge",
    )(x, mod, norm_g, hf, hb, og, yb, att, yd, w_mg, w_branch, w_out)


def _ffn_kernel(xm_ref, xp_ref, xn_ref, mod_ref, g_ref, wu_ref, wc_ref, wd_ref, o_ref, u_scr, rg_scr, rv_scr,
                h_scr, *, tm, seq):
    u = _modulated_tile(xm_ref, xp_ref, xn_ref, g_ref[0, 2:3, :], mod_ref[0, 0, 3:4, :], mod_ref[0, 0, 4:5, :],
                        tm, seq)
    u_scr[...] = u.astype(BF16)
    cw = FF_CHUNK
    for c in range(D_FF // cw):
        gcols = slice(c * cw, (c + 1) * cw)
        vcols = slice(D_FF + c * cw, D_FF + (c + 1) * cw)
        rg_scr[...] = _mm(u_scr[...], wu_ref[0, :, gcols])
        rv_scr[...] = _mm(u_scr[...], wu_ref[0, :, vcols])
        half = _conv3_taps(rg_scr, 0.5 * wc_ref[0, :, gcols], tm)
        val = _conv3(rv_scr, wc_ref, vcols, tm)
        h_scr[:, gcols] = ((half + half * jnp.tanh(half)) * val).astype(BF16)
    out = _mm(h_scr[...], wd_ref[0])
    x = xm_ref[0]
    o_ref[0] = x + mod_ref[0, 0, 5:6, :] * (_rms(out) * g_ref[0, 3:4, :])


def _ffn(x, mod, norm_g, w_up, w_conv, w_down, *, layer, mod_row, tm):
    bsz, seq, d = x.shape
    main, prev, nxt = _halo_specs(tm, d, seq)
    mrow = (lambda b: b) if mod_row is None else (lambda b: mod_row)
    lsel3 = lambda b, i: (layer, 0, 0)
    return pl.pallas_call(
        functools.partial(_ffn_kernel, tm=tm, seq=seq),
        grid=(bsz, seq // tm),
        in_specs=[main, prev, nxt,
                  pl.BlockSpec((1, 1, 6, d), lambda b, i: (layer, mrow(b), 0, 0)),
                  _const_spec((1, 4, d), lsel3),
                  _const_spec((1, d, 2 * D_FF), lsel3),
                  _const_spec((1, 3, 2 * D_FF), lsel3),
                  _const_spec((1, D_FF, d), lsel3)],
        out_specs=main,
        out_shape=jax.ShapeDtypeStruct((bsz, seq, d), F32),
        scratch_shapes=[pltpu.VMEM((tm + 2 * HALO, d), BF16), pltpu.VMEM((tm + 2 * HALO, FF_CHUNK), F32),
                        pltpu.VMEM((tm + 2 * HALO, FF_CHUNK), F32), pltpu.VMEM((tm, D_FF), BF16)],
        compiler_params=_params("parallel", "parallel"),
        name="convffn",
    )(x, x, x, mod, norm_g, w_up, w_conv, w_down)


def _rope_tables(seq):
    t = jnp.arange(seq, dtype=jnp.int32)[:, None]
    lane = jnp.arange(LANES, dtype=jnp.int32)[None, :]
    in_head = lane % ATT_HEAD_DIM
    n_freq = ATT_HEAD_DIM // 4
    inv_freq = ROPE_BASE ** (-(in_head % n_freq).astype(F32) / n_freq)
    pos = jnp.where(in_head < ATT_HEAD_DIM // 2, t // GRID_W, t % GRID_W).astype(F32)
    ang = pos * inv_freq
    sign = jnp.where((in_head % (2 * n_freq)) < n_freq, -1.0, 1.0)
    return jnp.cos(ang), sign * jnp.sin(ang)


def _tile_rows(seq):
    return min(seq, 512)


def kernel(x, c, ctx, c_ctx, w_mod, b_mod, norm_g, w_in, mlstm_qk_conv, mlstm_gate_bias, pool_w, pool_scale,
           attn_sink, sconv_w, w_branch, w_out, ffn_up, ffn_conv, ffn_down):
    bsz, seq, d = x.shape
    lc = ctx.shape[1]
    depth = w_mod.shape[0]
    assert d == D_MODEL and seq % 512 == 0 and lc % 256 == 0 and bsz + 1 <= 8

    cond = jnp.concatenate([c, c_ctx[None, :], jnp.zeros((8 - bsz - 1, d), F32)], axis=0)
    mod = _modulation(cond, w_mod, b_mod).reshape(depth, 8, 6, d)

    g0 = BRANCH_W * 4
    g1 = g0 + 4 * MLSTM_HEADS
    mg0 = w_in.shape[-1] - N_BRANCH * d
    p0 = g1
    aq0 = p0 + BRANCH_W
    ak0 = aq0 + BRANCH_W
    av0 = ak0 + ATT_KV_HEADS * ATT_HEAD_DIM
    sc0 = av0 + ATT_KV_HEADS * ATT_HEAD_DIM
    w_aq = w_in[:, :, aq0:ak0].reshape(depth, d, ATT_KV_HEADS, ATT_Q_HEADS // ATT_KV_HEADS, ATT_HEAD_DIM)
    w_aq = jnp.swapaxes(w_aq, 2, 3).reshape(depth, d, BRANCH_W)
    v0 = 2 * BRANCH_W
    w1 = jnp.concatenate([w_in[:, :, :v0], w_in[:, :, p0:aq0], w_aq, w_in[:, :, ak0:av0], w_in[:, :, sc0:mg0]],
                         axis=-1).astype(BF16)
    perm = jnp.array(_GATE_PERM)
    wt = jnp.concatenate([w_in[:, :, v0:g0], w_in[:, :, av0:sc0], w_in[:, :, g0:g1][:, :, perm]], axis=-1)
    wt = jnp.swapaxes(wt, 1, 2).astype(BF16)
    gb = mlstm_gate_bias.reshape(depth, 4 * MLSTM_HEADS)[:, perm][:, :, None]
    w_mg = w_in[:, :, mg0:].astype(BF16)
    pw = pool_w.astype(BF16)
    ps = pool_scale[:, None, :]
    sink_heads = attn_sink.reshape(depth, ATT_KV_HEADS, ATT_Q_HEADS // ATT_KV_HEADS).swapaxes(1, 2)
    sink = jnp.broadcast_to(sink_heads.reshape(depth, 1, ATT_Q_HEADS, 1),
                            (depth, 1, ATT_Q_HEADS, ATT_BLOCK)).reshape(depth, 1, ATT_Q_HEADS * ATT_BLOCK)
    wb = w_branch.astype(BF16)
    wo = w_out.astype(BF16)
    wu = ffn_up.astype(BF16)
    wc = ffn_conv
    wd = ffn_down.astype(BF16)

    cos_x, sin_x = _rope_tables(seq)
    cos_c, sin_c = jnp.ones((lc, LANES), F32), jnp.zeros((lc, LANES), F32)

    hd = MLSTM_HD
    zero_c = jnp.zeros((bsz, MLSTM_HEADS, 2 * hd, hd), F32)
    zero_m = jnp.zeros((bsz, 8, LANES), F32)
    tm_x, tm_c = _tile_rows(seq), _tile_rows(lc)
    scan = functools.partial(_mlstm, chunk=MLSTM_CHUNK)

    h = ctx
    for l in range(depth):
        ctx_out = l < depth - 1
        proj = functools.partial(_inproj, mod=mod, norm_g=norm_g, w1=w1, wt=wt, gb=gb, qkc=mlstm_qk_conv,
                                 scw=sconv_w, pw=pw, ps=ps, layer=l)
        qk_c, v_c, og_c, gt_c, yb_c, aq_c, ak_c, avt_c, yd_c = proj(h, cos=cos_c, sin=sin_c, mod_row=bsz, tm=tm_c)
        qk_x, v_x, og_x, gt_x, yb_x, aq_x, ak_x, avt_x, yd_x = proj(x, cos=cos_x, sin=sin_x, mod_row=None, tm=tm_x)
        hf_c, cf, mf = scan(qk_c, v_c, gt_c, zero_c, zero_m, reverse=False)
        hb_c, cb, mb = scan(qk_c, v_c, gt_c, zero_c, zero_m, reverse=True)
        hf_x, _, _ = scan(qk_x, v_x, gt_x, cf, mf, reverse=False)
        hb_x, _, _ = scan(qk_x, v_x, gt_x, cb, mb, reverse=True)
        att_x = _attention(aq_x, ak_x, avt_x, ak_c, avt_c, sink, layer=l, local=True)
        tail = dict(norm_g=norm_g, layer=l)
        x = _merge(x, mod, hf=hf_x, hb=hb_x, og=og_x, yb=yb_x, att=att_x, yd=yd_x, w_mg=w_mg, w_branch=wb,
                   w_out=wo, mod_row=None, tm=tm_x, **tail)
        x = _ffn(x, mod, w_up=wu, w_conv=wc, w_down=wd, mod_row=None, tm=tm_x, **tail)
        if ctx_out:
            att_c = _attention(aq_c, ak_c, avt_c, ak_c, avt_c, sink, layer=l, local=False)
            h = _merge(h, mod, hf=hf_c, hb=hb_c, og=og_c, yb=yb_c, att=att_c, yd=yd_c, w_mg=w_mg, w_branch=wb,
                       w_out=wo, mod_row=bsz, tm=tm_c, **tail)
            h = _ffn(h, mod, w_up=wu, w_conv=wc, w_down=wd, mod_row=bsz, tm=tm_c, **tail)
    return x
```
